```python
import math
import jax, jax.numpy as jnp
from jax import lax
import numpy as np

D_MODEL = 1024
BATCH = 4
SEQ = 4096
DEPTH = 4

HEAD_DIM = 64
N_MIXERS = 4
N_HEADS_TOTAL = D_MODEL // HEAD_DIM
G = N_HEADS_TOTAL // N_MIXERS
HW = G * HEAD_DIM
MIX_WIDTH = N_MIXERS * HW
Q_BLOCK = 128
MOBA_BLOCK = 256
MOBA_TOPK = 3
MOBA_QCHUNK = 64
MLA_Q_RANK = D_MODEL // 4
MLA_KV_RANK = D_MODEL // 8
MLA_NOPE_DIM = HEAD_DIM
MLA_ROPE_DIM = HEAD_DIM // 2
MLA_V_DIM = HEAD_DIM
ROPE_THETA = 10000.0
DIL_PAIRS = ((128, 1), (512, 4), (2048, 16))
N_ALIBI_HEADS = 2 * G
D_FF = 256 * (-(-(8 * D_MODEL) // (3 * 256)))
RMS_EPS = 1e-6
NEG_INF = -1e30
MIX_SPLIT_WIDTHS = (HW, HW, HW, HW, HW, HW, G, MLA_Q_RANK, MLA_KV_RANK, MLA_ROPE_DIM, HW, HW, HW)
IN_WIDTH = sum(MIX_SPLIT_WIDTHS)
SPLIT_POINTS = tuple(int(v) for v in np.cumsum(MIX_SPLIT_WIDTHS)[:-1])

kernel_name = 'hymba_style_moba_fox_mla_dilated_macaron'


def _rmsnorm(x, g):
    xf = x.astype(jnp.float32)
    y = xf * lax.rsqrt(jnp.mean(xf * xf, axis=-1, keepdims=True) + RMS_EPS)
    return (y * g.astype(jnp.float32)).astype(x.dtype)


def _swiglu(h, w_gate, w_up, w_down):
    return (jax.nn.silu(h @ w_gate) * (h @ w_up)) @ w_down


def _to_heads(a):
    B, T, _ = a.shape
    return a.reshape(B, T, G, -1).transpose(0, 2, 1, 3)


def _from_heads(a):
    B, H, T, d = a.shape
    return a.transpose(0, 2, 1, 3).reshape(B, T, H * d)


def _alibi_slopes():
    s = np.array([2.0 ** (-8.0 * (i + 1) / N_ALIBI_HEADS) for i in range(N_ALIBI_HEADS)], dtype=np.float32)
    return jnp.asarray(s[1::2]), jnp.asarray(s[0::2])


def _rope_tables(T):
    inv = ROPE_THETA ** (-jnp.arange(0, MLA_ROPE_DIM, 2, dtype=jnp.float32) / MLA_ROPE_DIM)
    ang = jnp.arange(T, dtype=jnp.float32)[:, None] * inv[None, :]
    return jnp.cos(ang), jnp.sin(ang)


def _rope(x, cos, sin):
    half = x.shape[-1] // 2
    x1, x2 = x[..., :half], x[..., half:]
    c = cos[None, :, None, :].astype(x.dtype)
    s = sin[None, :, None, :].astype(x.dtype)
    return jnp.concatenate([x1 * c - x2 * s, x1 * s + x2 * c], axis=-1)


def _causal_block_attention(q, k, v, decay=None):
    B, H, T, dq = q.shape
    scale = dq ** -0.5
    kpos = jnp.arange(T)

    def block(start):
        qb = lax.dynamic_slice_in_dim(q, start, Q_BLOCK, axis=2)
        t = start + jnp.arange(Q_BLOCK)
        s = jnp.einsum('bhqd,bhkd->bhqk', qb, k).astype(jnp.float32) * scale
        if decay is not None:
            s = s + lax.dynamic_slice_in_dim(decay, start, Q_BLOCK, axis=2)[..., None] - decay[:, :, None, :]
        s = jnp.where(kpos[None, :] <= t[:, None], s, NEG_INF)
        p = jax.nn.softmax(s, axis=-1).astype(v.dtype)
        return jnp.einsum('bhqk,bhkd->bhqd', p, v)

    out = lax.map(block, jnp.arange(T // Q_BLOCK) * Q_BLOCK)
    return jnp.moveaxis(out, 0, 2).reshape(B, H, T, v.shape[-1])


def _moba_attention(q, k, v, slopes):
    B, H, T, hd = q.shape
    blk = MOBA_BLOCK
    Tp = -(-T // blk) * blk
    pad = ((0, 0), (0, 0), (0, Tp - T), (0, 0))
    q, k, v = jnp.pad(q, pad), jnp.pad(k, pad), jnp.pad(v, pad)
    nb = Tp // blk
    kb = k.reshape(B, H, nb, blk, hd)
    vb = v.reshape(B, H, nb, blk, hd)
    kmean = jnp.mean(kb.astype(jnp.float32), axis=3)
    topk = min(MOBA_TOPK, nb)
    scale = hd ** -0.5
    bi = jnp.arange(B)[:, None, None, None]
    hi = jnp.arange(H)[None, :, None, None]
    blk_ids = jnp.arange(nb)
    offs = jnp.arange(blk)

    def chunk(start):
        qc = lax.dynamic_slice_in_dim(q, start, MOBA_QCHUNK, axis=2)
        t = start + jnp.arange(MOBA_QCHUNK)
        cur = start // blk
        gate = jnp.einsum('bhqd,bhnd->bhqn', qc.astype(jnp.float32), kmean)
        gate = jnp.where(blk_ids < cur, gate, NEG_INF)
        idx = lax.top_k(gate, topk)[1]
        k_sel = kb[bi, hi, idx]
        v_sel = vb[bi, hi, idx]
        dist_sel = t[:, None, None] - (idx[..., None] * blk + offs)
        s_sel = (jnp.einsum('bhqd,bhqjkd->bhqjk', qc, k_sel).astype(jnp.float32) * scale
                 - slopes[:, None, None, None] * dist_sel)
        s_sel = jnp.where((idx < cur)[..., None], s_sel, NEG_INF)
        k_own = lax.dynamic_slice_in_dim(k, cur * blk, blk, axis=2)
        v_own = lax.dynamic_slice_in_dim(v, cur * blk, blk, axis=2)
        dist_own = t[:, None] - (cur * blk + offs)[None, :]
        s_own = (jnp.einsum('bhqd,bhkd->bhqk', qc, k_own).astype(jnp.float32) * scale
                 - slopes[:, None, None] * dist_own)
        s_own = jnp.where(dist_own >= 0, s_own, NEG_INF)
        s = jnp.concatenate([s_sel.reshape(B, H, MOBA_QCHUNK, topk * blk), s_own], axis=-1)
        p = jax.nn.softmax(s, axis=-1).astype(v.dtype)
        p_sel = p[..., :topk * blk].reshape(B, H, MOBA_QCHUNK, topk, blk)
        p_own = p[..., topk * blk:]
        return (jnp.einsum('bhqjk,bhqjkd->bhqd', p_sel, v_sel)
                + jnp.einsum('bhqk,bhkd->bhqd', p_own, v_own))

    out = lax.map(chunk, jnp.arange(Tp // MOBA_QCHUNK) * MOBA_QCHUNK)
    return jnp.moveaxis(out, 0, 2).reshape(B, H, Tp, hd)[:, :, :T]


def _mla_attention(c_q, c_kv, k_rope, q_norm, kv_norm, w_uq, w_ukv, cos, sin):
    B, T, _ = c_q.shape
    q = (_rmsnorm(c_q, q_norm) @ w_uq).reshape(B, T, G, MLA_NOPE_DIM + MLA_ROPE_DIM)
    kv = (_rmsnorm(c_kv, kv_norm) @ w_ukv).reshape(B, T, G, MLA_NOPE_DIM + MLA_V_DIM)
    q_nope, q_rot = q[..., :MLA_NOPE_DIM], q[..., MLA_NOPE_DIM:]
    k_nope, v = kv[..., :MLA_NOPE_DIM], kv[..., MLA_NOPE_DIM:]
    q_rot = _rope(q_rot, cos, sin)
    k_rot = jnp.broadcast_to(_rope(k_rope[:, :, None, :], cos, sin), (B, T, G, MLA_ROPE_DIM))
    qh = jnp.concatenate([q_nope, q_rot], axis=-1).transpose(0, 2, 1, 3)
    kh = jnp.concatenate([k_nope, k_rot], axis=-1).transpose(0, 2, 1, 3)
    return _causal_block_attention(qh, kh, v.transpose(0, 2, 1, 3))


def _dilated_branch(q, k, v, slopes, window, dil):
    B, H, T, hd = q.shape
    span = window // dil
    L = T // dil
    nb = -(-L // span)
    Lp = nb * span

    def by_residue(a):
        a = a.reshape(B, H, L, dil, hd).transpose(0, 1, 3, 2, 4)
        a = jnp.pad(a, ((0, 0), (0, 0), (0, 0), (0, Lp - L), (0, 0)))
        return a.reshape(B, H, dil, nb, span, hd)

    def with_prev(a):
        prev = jnp.pad(a, ((0, 0), (0, 0), (0, 0), (1, 0), (0, 0), (0, 0)))[:, :, :, :-1]
        return jnp.concatenate([prev, a], axis=4)

    qb = by_residue(q)
    kb = with_prev(by_residue(k))
    vb = with_prev(by_residue(v))
    qi = jnp.arange(span)[:, None]
    kj = jnp.arange(2 * span)[None, :]
    delta = span + qi - kj
    key_idx = (jnp.arange(nb)[:, None, None] - 1) * span + kj[None]
    valid = (delta >= 0) & (delta <= span) & (key_idx >= 0)
    s = (jnp.einsum('bhrnqd,bhrnkd->bhrnqk', qb, kb).astype(jnp.float32) * hd ** -0.5
         - slopes[:, None, None, None, None] * (delta * dil).astype(jnp.float32))
    s = jnp.where(valid, s, NEG_INF)
    lse = jax.nn.logsumexp(s, axis=-1)
    p = jnp.exp(s - lse[..., None]).astype(v.dtype)
    o = jnp.einsum('bhrnqk,bhrnkd->bhrnqd', p, vb)
    o = o.reshape(B, H, dil, Lp, hd)[:, :, :, :L].transpose(0, 1, 3, 2, 4).reshape(B, H, T, hd)
    lse = lse.reshape(B, H, dil, Lp)[..., :L].transpose(0, 1, 3, 2).reshape(B, H, T)
    return o, lse


def _dilated_attention(q, k, v, slopes):
    branches = [_dilated_branch(q, k, v, slopes, w, d) for (w, d) in DIL_PAIRS]
    o = jnp.stack([b[0] for b in branches]).astype(jnp.float32)
    wts = jax.nn.softmax(jnp.stack([b[1] for b in branches]), axis=0)
    return jnp.sum(wts[..., None] * o, axis=0).astype(q.dtype)


def setup_inputs(seed: int = 0) -> dict:
    key = jax.random.key(seed)
    ks = jax.random.split(key, 18)
    f32 = jnp.float32
    L, D = DEPTH, D_MODEL

    def w(k, shape, fan_in):
        return jax.random.normal(k, shape, f32) * fan_in ** -0.5

    def gain(k, shape):
        return 1.0 + 0.1 * jax.random.normal(k, shape, f32)

    return {
        'x': jax.random.normal(ks[0], (BATCH, SEQ, D), f32),
        'norm_ffn1': gain(ks[1], (L, D)),
        'ffn1_gate': w(ks[2], (L, D, D_FF), D),
        'ffn1_up': w(ks[3], (L, D, D_FF), D),
        'ffn1_down': w(ks[4], (L, D_FF, D), D_FF),
        'norm_mix': gain(ks[5], (L, D)),
        'w_in': w(ks[6], (L, D, IN_WIDTH), D),
        'forget_bias': jax.random.uniform(ks[7], (L, G), f32, 1.0, 4.0),
        'mla_q_norm': gain(ks[8], (L, MLA_Q_RANK)),
        'mla_kv_norm': gain(ks[9], (L, MLA_KV_RANK)),
        'w_uq': w(ks[10], (L, MLA_Q_RANK, G * (MLA_NOPE_DIM + MLA_ROPE_DIM)), MLA_Q_RANK),
        'w_ukv': w(ks[11], (L, MLA_KV_RANK, G * (MLA_NOPE_DIM + MLA_V_DIM)), MLA_KV_RANK),
        'w_out': w(ks[12], (L, MIX_WIDTH, D), MIX_WIDTH),
        'norm_ffn2': gain(ks[13], (L, D)),
        'ffn2_gate': w(ks[14], (L, D, D_FF), D),
        'ffn2_up': w(ks[15], (L, D, D_FF), D),
        'ffn2_down': w(ks[16], (L, D_FF, D), D_FF),
        'norm_final': gain(ks[17], (D,)),
    }


def reference(x, norm_ffn1, ffn1_gate, ffn1_up, ffn1_down, norm_mix, w_in, forget_bias,
              mla_q_norm, mla_kv_norm, w_uq, w_ukv, w_out, norm_ffn2, ffn2_gate, ffn2_up,
              ffn2_down, norm_final):
    T = x.shape[1]
    slopes_moba, slopes_dil = _alibi_slopes()
    cos, sin = _rope_tables(T)
    for l in range(DEPTH):
        x = x + 0.5 * _swiglu(_rmsnorm(x, norm_ffn1[l]), ffn1_gate[l], ffn1_up[l], ffn1_down[l])
        h = _rmsnorm(x, norm_mix[l])
        (aq, ak, av, bq, bk, bv, bf, cq, ckv, ckr, dq, dk, dv) = jnp.split(h @ w_in[l], SPLIT_POINTS, axis=-1)
        o_a = _moba_attention(_to_heads(aq), _to_heads(ak), _to_heads(av), slopes_moba)
        logf = jax.nn.log_sigmoid((bf + forget_bias[l]).astype(jnp.float32)).transpose(0, 2, 1)
        o_b = _causal_block_attention(_to_heads(bq), _to_heads(bk), _to_heads(bv),
                                      decay=jnp.cumsum(logf, axis=-1))
        o_c = _mla_attention(cq, ckv, ckr, mla_q_norm[l], mla_kv_norm[l], w_uq[l], w_ukv[l], cos, sin)
        o_d = _dilated_attention(_to_heads(dq), _to_heads(dk), _to_heads(dv), slopes_dil)
        mixed = jnp.concatenate([_from_heads(o_a), _from_heads(o_b), _from_heads(o_c), _from_heads(o_d)], axis=-1)
        x = x + mixed @ w_out[l]
        x = x + 0.5 * _swiglu(_rmsnorm(x, norm_ffn2[l]), ffn2_gate[l], ffn2_up[l], ffn2_down[l])
    return _rmsnorm(x, norm_final)
```

```python
import functools

import numpy as np
import jax
import jax.numpy as jnp
from jax import lax
from jax.experimental import pallas as pl
from jax.experimental.pallas import tpu as pltpu

F32 = jnp.float32
BF16 = jnp.bfloat16

D_MODEL = 1024
HEAD_DIM = 64
N_GROUP = 4
HW = N_GROUP * HEAD_DIM
LANES = 128
MOBA_BLOCK = 256
MOBA_TOPK = 3
MLA_Q_RANK = 256
MLA_KV_RANK = 128
MLA_ROPE = 32
ROPE_THETA = 10000.0
DIL_PAIRS = ((128, 1), (512, 4), (2048, 16))
DIL_SPAN = 128
D_FF = 2816
RMS_EPS = 1e-6
NEG = -1e30
LOG2E = 1.4426950408889634

SEL0 = 64
ONE_Q0 = 80
PIECE_Q0 = 83
N_PIECE = 3

TILE = 512
FFN_TILE = 512
FF_CHUNK = 256
VMEM_LIMIT = 56 * 1024 * 1024

_ALIBI = [2.0 ** (-8.0 * (i + 1) / 8) for i in range(8)]
SLOPES_MOBA = _ALIBI[1::2]
SLOPES_DIL = _ALIBI[0::2]

_WIDTHS = (HW, HW, HW, HW, HW, HW, N_GROUP, MLA_Q_RANK, MLA_KV_RANK, MLA_ROPE, HW, HW, HW)
_OFF = np.concatenate([[0], np.cumsum(_WIDTHS)]).astype(np.int64)
(AQ0, AK0, AV0, BQ0, BK0, BV0, BF0, CQ0, CKV0, CKR0, DQ0, DK0, DV0, IN_WIDTH) = [int(v) for v in _OFF]


def _dot(a, b):
    return jnp.dot(a, b, preferred_element_type=F32)


def _dot_nt(a, b):
    return lax.dot_general(a, b, (((1,), (1,)), ((), ())), preferred_element_type=F32)


def _rms(x, g):
    return x * lax.rsqrt(jnp.mean(x * x, axis=-1, keepdims=True) + RMS_EPS) * g


def _split3(x):
    p0 = x.astype(BF16)
    r = x - p0.astype(F32)
    p1 = r.astype(BF16)
    p2 = (r - p1.astype(F32)).astype(BF16)
    return p0, p1, p2


def _log_sigmoid(x):
    return jnp.minimum(x, 0.0) - jnp.log(1.0 + jnp.exp(-jnp.abs(x)))


def _ffn_kernel(*refs, has_mix, final_norm):
    it = iter(refs)
    x_ref = next(it)
    if has_mix:
        o_refs = [next(it) for _ in range(4)]
        wo_ref = next(it)
    g_ref, wg_ref, wu_ref, wd_ref = next(it), next(it), next(it), next(it)
    gf_ref = next(it) if final_norm else None
    out_ref = next(it)
    act_ref = next(it)

    x = x_ref[...]
    if has_mix:
        for i, o_ref in enumerate(o_refs):
            x = x + _dot(o_ref[...], wo_ref[i * HW:(i + 1) * HW, :])
    h = _rms(x, g_ref[...]).astype(BF16)
    for c in range(D_FF // FF_CHUNK):
        sl = slice(c * FF_CHUNK, (c + 1) * FF_CHUNK)
        gate = _dot(h, wg_ref[:, sl])
        up = _dot(h, wu_ref[:, sl])
        act_ref[:, sl] = (gate * jax.nn.sigmoid(gate) * up).astype(BF16)
    x = x + 0.5 * _dot(act_ref[...], wd_ref[...])
    if final_norm:
        x = _rms(x, gf_ref[...])
    out_ref[...] = x


def _const_spec(shape):
    nd = len(shape)
    return pl.BlockSpec(shape, lambda *_: (0,) * nd, pipeline_mode=pl.Buffered(1))


def _ffn_call(x, mix, w_out, g, wg, wu, wd, g_final):
    n = x.shape[0]
    tm = FFN_TILE
    has_mix = mix is not None
    final_norm = g_final is not None
    row = lambda i: (i, 0)
    args, specs = [x], [pl.BlockSpec((tm, D_MODEL), row)]
    if has_mix:
        for o in mix:
            args.append(o)
            specs.append(pl.BlockSpec((tm, HW), row))
        args.append(w_out)
        specs.append(_const_spec((D_MODEL, D_MODEL)))
    args += [g, wg, wu, wd]
    specs += [_const_spec((1, D_MODEL)), _const_spec((D_MODEL, D_FF)), _const_spec((D_MODEL, D_FF)),
              _const_spec((D_FF, D_MODEL))]
    if final_norm:
        args.append(g_final)
        specs.append(_const_spec((1, D_MODEL)))
    return pl.pallas_call(
        functools.partial(_ffn_kernel, has_mix=has_mix, final_norm=final_norm),
        out_shape=jax.ShapeDtypeStruct((n, D_MODEL), F32),
        grid=(n // tm,),
        in_specs=specs,
        out_specs=pl.BlockSpec((tm, D_MODEL), row),
        scratch_shapes=[pltpu.VMEM((tm, D_FF), BF16)],
        compiler_params=pltpu.CompilerParams(dimension_semantics=("arbitrary",),
                                             vmem_limit_bytes=VMEM_LIMIT),
        name="ffn_mix" if has_mix else "ffn",
    )(*args)


NN_AQ, NN_AV, NN_BQ, NN_BV, NN_D, NN_CQ, NN_CKV, NN_END = 0, 512, 1024, 1536, 2048, 2816, 3072, 3200
NT_AK, NT_BK, NT_KR, NT_KRH, NT_END = 0, 512, 1024, 1152, 1280


def _pieces_select(idx, base, val):
    p0, p1, p2 = _split3(val)
    return jnp.where(idx == base, p0.astype(F32),
                     jnp.where(idx == base + 1, p1.astype(F32), p2.astype(F32)))


def _with_ones(v, head):
    lane = lax.broadcasted_iota(jnp.int32, v.shape, 1)
    keep = (lane < HEAD_DIM) if head % 2 == 0 else (lane >= HEAD_DIM)
    return jnp.where(keep, v, 1.0).astype(BF16)


def _proj_kernel(x_ref, g_ref, wnn_ref, wnt_ref, fbq_ref, fbk_ref, tri_ref, trit_ref,
                 qn_ref, kvn_ref, wuq_ref, wuqrh_ref, wknt_ref, wv_ref,
                 cosq_ref, sinq_ref, cosk_ref, sink_ref,
                 qa_a, ka_a, va_a, qa_b, ka_b, va_b, qa_c, ka_c, va_c, yd_ref,
                 carry_q, carry_k, *, tm):
    i = pl.program_id(1)
    h = _rms(x_ref[0], g_ref[...]).astype(BF16)
    lane = lax.broadcasted_iota(jnp.int32, (tm, LANES), 1)
    rowi = lax.broadcasted_iota(jnp.int32, (LANES, tm), 0)
    tok_q = (i * tm + lax.broadcasted_iota(jnp.int32, (tm, LANES), 0)).astype(F32)
    tok_k_i = i * tm + lax.broadcasted_iota(jnp.int32, (LANES, tm), 1)
    tok_k = tok_k_i.astype(F32)
    ones_q = (lane >= ONE_Q0) & (lane < ONE_Q0 + N_PIECE)
    piece_q = (lane >= PIECE_Q0) & (lane < PIECE_Q0 + N_PIECE)
    piece_k = (rowi >= ONE_Q0) & (rowi < ONE_Q0 + N_PIECE)
    ones_k = (rowi >= PIECE_Q0) & (rowi < PIECE_Q0 + N_PIECE)
    c_hd = HEAD_DIM ** -0.5 * LOG2E

    @pl.when(i == 0)
    def _():
        carry_q[...] = jnp.zeros_like(carry_q)
        carry_k[...] = jnp.zeros_like(carry_k)

    for hd in range(N_GROUP):
        sl = slice(hd * LANES, (hd + 1) * LANES)
        slope2 = SLOPES_MOBA[hd] * LOG2E
        z = _dot(h, wnn_ref[:, NN_AQ + hd * LANES:NN_AQ + (hd + 1) * LANES])
        ext = _pieces_select(lane, PIECE_Q0, tok_q * (-slope2))
        qa_a[0, hd] = jnp.where(lane < HEAD_DIM, z * c_hd,
                                jnp.where(ones_q, 1.0, jnp.where(piece_q, ext, 0.0))).astype(BF16)
        zt = _dot_nt(wnt_ref[NT_AK + hd * LANES:NT_AK + (hd + 1) * LANES, :], h)
        ext = _pieces_select(rowi, ONE_Q0, tok_k * slope2)
        ind = (rowi >= SEL0) & (rowi < SEL0 + 16) & ((tok_k_i // MOBA_BLOCK) == rowi - SEL0)
        ka_a[0, hd, 0] = jnp.where(rowi < HEAD_DIM, zt,
                                   jnp.where(piece_k, ext,
                                             jnp.where(ones_k | ind, 1.0, 0.0))).astype(BF16)
        v = _dot(h, wnn_ref[:, NN_AV + hd * LANES:NN_AV + (hd + 1) * LANES])
        va_a[0, hd] = _with_ones(v, hd)

    for hd in range(N_GROUP):
        z = _dot(h, wnn_ref[:, NN_BQ + hd * LANES:NN_BQ + (hd + 1) * LANES])
        lf = jnp.where(piece_q, _log_sigmoid(z + fbq_ref[hd]), 0.0)
        p0, p1, p2 = _split3(lf)
        tri = tri_ref[...]
        dec = _dot(tri, p0) + _dot(tri, p1) + _dot(tri, p2) + carry_q[hd, 0:1, :]
        carry_q[hd, 0:1, :] = dec[tm - 1:tm, :]
        ext = _pieces_select(lane, PIECE_Q0, dec * LOG2E)
        qa_b[0, hd] = jnp.where(lane < HEAD_DIM, z * c_hd,
                                jnp.where(ones_q, 1.0, jnp.where(piece_q, ext, 0.0))).astype(BF16)
        zt = _dot_nt(wnt_ref[NT_BK + hd * LANES:NT_BK + (hd + 1) * LANES, :], h)
        lft = jnp.where(piece_k, _log_sigmoid(zt + fbk_ref[hd]), 0.0)
        p0, p1, p2 = _split3(lft)
        trit = trit_ref[...]
        dect = _dot(p0, trit) + _dot(p1, trit) + _dot(p2, trit) + carry_k[hd, :, 0:1]
        carry_k[hd, :, 0:1] = dect[:, tm - 1:tm]
        ext = _pieces_select(rowi, ONE_Q0, dect * (-LOG2E))
        ka_b[0, hd, 0] = jnp.where(rowi < HEAD_DIM, zt,
                                   jnp.where(piece_k, ext, jnp.where(ones_k, 1.0, 0.0))).astype(BF16)
        v = _dot(h, wnn_ref[:, NN_BV + hd * LANES:NN_BV + (hd + 1) * LANES])
        va_b[0, hd] = _with_ones(v, hd)

    c_mla = (HEAD_DIM + MLA_ROPE) ** -0.5 * LOG2E
    cqn = _rms(_dot(h, wnn_ref[:, NN_CQ:NN_CKV]), qn_ref[...]).astype(BF16)
    ckvn = _rms(_dot(h, wnn_ref[:, NN_CKV:NN_END]), kvn_ref[...]).astype(BF16)
    cosq, sinq = cosq_ref[...], sinq_ref[...]
    krt = _dot_nt(wnt_ref[NT_KR:NT_KRH, :], h)
    krht = _dot_nt(wnt_ref[NT_KRH:NT_END, :], h)
    rot_k = krt * cosk_ref[...] + krht * sink_ref[...]
    for hd in range(N_GROUP):
        sl = slice(hd * LANES, (hd + 1) * LANES)
        q = _dot(cqn, wuq_ref[:, sl])
        qrh = _dot(cqn, wuqrh_ref[:, sl])
        qa_c[0, hd] = ((q * cosq + qrh * sinq) * c_mla).astype(BF16)
        knt = _dot_nt(wknt_ref[sl, :], ckvn)
        ka_c[0, hd, 0] = (knt + rot_k).astype(BF16)
        va_c[0, hd] = _with_ones(_dot(ckvn, wv_ref[:, sl]), hd)

    yd_ref[0] = _dot(h, wnn_ref[:, NN_D:NN_CQ])


def _proj_call(x3, g, wnn, wnt, fbq, fbk, tri, trit, qn, kvn, wuq, wuqrh, wknt, wv,
               cosq, sinq, cosk, sink):
    b, t, _ = x3.shape
    tm = TILE
    nk = t // tm
    qa_shape = jax.ShapeDtypeStruct((b, N_GROUP, t, LANES), BF16)
    ka_shape = jax.ShapeDtypeStruct((b, N_GROUP, nk, LANES, tm), BF16)
    qa_spec = pl.BlockSpec((1, N_GROUP, tm, LANES), lambda bi, i: (bi, 0, i, 0))
    ka_spec = pl.BlockSpec((1, N_GROUP, 1, LANES, tm), lambda bi, i: (bi, 0, i, 0, 0))
    in_specs = [
        pl.BlockSpec((1, tm, D_MODEL), lambda bi, i: (bi, i, 0)),
        _const_spec((1, D_MODEL)),
        _const_spec((D_MODEL, NN_END)),
        _const_spec((NT_END, D_MODEL)),
        _const_spec((N_GROUP, 1, LANES)),
        _const_spec((N_GROUP, LANES, 1)),
        _const_spec((tm, tm)),
        _const_spec((tm, tm)),
        _const_spec((1, MLA_Q_RANK)),
        _const_spec((1, MLA_KV_RANK)),
        _const_spec((MLA_Q_RANK, N_GROUP * LANES)),
        _const_spec((MLA_Q_RANK, N_GROUP * LANES)),
        _const_spec((N_GROUP * LANES, MLA_KV_RANK)),
        _const_spec((MLA_KV_RANK, N_GROUP * LANES)),
        pl.BlockSpec((tm, LANES), lambda bi, i: (i, 0)),
        pl.BlockSpec((tm, LANES), lambda bi, i: (i, 0)),
        pl.BlockSpec((LANES, tm), lambda bi, i: (0, i)),
        pl.BlockSpec((LANES, tm), lambda bi, i: (0, i)),
    ]
    out_shape = [qa_shape, ka_shape, qa_shape] * 3 + [jax.ShapeDtypeStruct((b, t, 3 * HW), F32)]
    out_specs = [qa_spec, ka_spec, qa_spec] * 3 + [pl.BlockSpec((1, tm, 3 * HW), lambda bi, i: (bi, i, 0))]
    return pl.pallas_call(
        functools.partial(_proj_kernel, tm=tm),
        out_shape=out_shape,
        grid=(b, nk),
        in_specs=in_specs,
        out_specs=out_specs,
        scratch_shapes=[pltpu.VMEM((N_GROUP, 8, LANES), F32), pltpu.VMEM((N_GROUP, LANES, LANES), F32)],
        compiler_params=pltpu.CompilerParams(dimension_semantics=("arbitrary", "arbitrary"),
                                             vmem_limit_bytes=VMEM_LIMIT),
        name="proj",
    )(x3, g, wnn, wnt, fbq, fbk, tri, trit, qn, kvn, wuq, wuqrh, wknt, wv, cosq, sinq, cosk, sink)


def _gate_kernel(qa0_ref, ka_ref, qa_ref, *, t, tk):
    nk = t // tk
    nblk = t // MOBA_BLOCK
    blk = lax.broadcasted_iota(jnp.int32, (nblk, t), 0)
    cur = lax.broadcasted_iota(jnp.int32, (nblk, t), 1) // MOBA_BLOCK
    valid = blk < cur
    for hd in range(N_GROUP):
        ksum = jnp.zeros((nblk, LANES), F32)
        for j in range(nk):
            key_blk = (j * tk + lax.broadcasted_iota(jnp.int32, (nblk, tk), 1)) // MOBA_BLOCK
            ind = jnp.where(key_blk == lax.broadcasted_iota(jnp.int32, (nblk, tk), 0), 1.0, 0.0)
            ksum = ksum + _dot_nt(ind.astype(BF16), ka_ref[0, hd, j])
        lane = lax.broadcasted_iota(jnp.int32, (nblk, LANES), 1)
        kmean = jnp.where(lane < HEAD_DIM, ksum * (1.0 / MOBA_BLOCK), 0.0).astype(BF16)
        q0 = qa0_ref[0, hd]
        gate = jnp.where(valid, _dot_nt(kmean, q0), -jnp.inf)
        rank = jnp.zeros((nblk, t), F32)
        for m in range(nblk):
            gm = gate[m:m + 1, :]
            ahead = (gm > gate) | ((gm == gate) & (blk > m))
            rank = rank + jnp.where(ahead, 1.0, 0.0)
        keep = (valid & (rank < MOBA_TOPK)) | (blk == cur)
        selb = jnp.where(keep, 0.0, NEG)
        full = jnp.concatenate([jnp.zeros((SEL0, t), F32), selb,
                                jnp.zeros((LANES - SEL0 - nblk, t), F32)], axis=0)
        qa_ref[0, hd] = (q0.astype(F32) + full.T).astype(BF16)


def _gate_call(qa0, ka):
    b, _, t, _ = qa0.shape
    tk = ka.shape[-1]
    assert (t // MOBA_BLOCK) % 8 == 0 and t // MOBA_BLOCK <= ONE_Q0 - SEL0
    qa_spec = pl.BlockSpec((1, N_GROUP, t, LANES), lambda bi: (bi, 0, 0, 0))
    ka_spec = pl.BlockSpec((1, N_GROUP, t // tk, LANES, tk), lambda bi: (bi, 0, 0, 0, 0))
    return pl.pallas_call(
        functools.partial(_gate_kernel, t=t, tk=tk),
        out_shape=jax.ShapeDtypeStruct(qa0.shape, BF16),
        grid=(b,),
        in_specs=[qa_spec, ka_spec],
        out_specs=qa_spec,
        compiler_params=pltpu.CompilerParams(dimension_semantics=("arbitrary",),
                                             vmem_limit_bytes=VMEM_LIMIT),
        name="moba_gate",
    )(qa0, ka)


def _attn_kernel(qa_ref, ka_ref, va_ref, o_ref, *, tq):
    qi = pl.program_id(1)
    tk = tq
    row = lax.broadcasted_iota(jnp.int32, (tq, tk), 0)
    col = lax.broadcasted_iota(jnp.int32, (tq, tk), 1)
    lane = lax.broadcasted_iota(jnp.int32, (tq, LANES), 1)

    def head(hd):
        q = qa_ref[0, hd]

        def update(j, m, acc, diag):
            s = _dot(q, ka_ref[0, hd, j])
            if diag:
                s = jnp.where(col <= row, s, NEG)
            m_new = jnp.maximum(m, jnp.max(s, axis=1, keepdims=True))
            p = jnp.exp2(s - m_new)
            acc = jnp.exp2(m - m_new) * acc + _dot(p.astype(BF16), va_ref[0, hd, pl.ds(j * tk, tk), :])
            return m_new, acc

        m0 = jnp.full((tq, 1), -jnp.inf, F32)
        acc0 = jnp.zeros((tq, LANES), F32)
        m, acc = lax.fori_loop(0, qi, lambda j, c: update(j, c[0], c[1], False), (m0, acc0))
        _, acc = update(qi, m, acc, True)
        denom = acc[:, HEAD_DIM:HEAD_DIM + 1] if hd % 2 == 0 else acc[:, 0:1]
        return acc / denom

    for pair in range(N_GROUP // 2):
        even, odd = head(2 * pair), head(2 * pair + 1)
        o_ref[0, :, pair * LANES:(pair + 1) * LANES] = jnp.where(lane < HEAD_DIM, even, odd).astype(BF16)


def _attn_call(qa, ka, va, name):
    b, _, t, _ = qa.shape
    tq = ka.shape[-1]
    nk = t // tq
    return pl.pallas_call(
        functools.partial(_attn_kernel, tq=tq),
        out_shape=jax.ShapeDtypeStruct((b, t, HW), BF16),
        grid=(b, nk),
        in_specs=[
            pl.BlockSpec((1, N_GROUP, tq, LANES), lambda bi, i: (bi, 0, i, 0)),
            pl.BlockSpec((1, N_GROUP, nk, LANES, tq), lambda bi, i: (bi, 0, 0, 0, 0)),
            pl.BlockSpec((1, N_GROUP, t, LANES), lambda bi, i: (bi, 0, 0, 0)),
        ],
        out_specs=pl.BlockSpec((1, tq, HW), lambda bi, i: (bi, i, 0)),
        compiler_params=pltpu.CompilerParams(dimension_semantics=("arbitrary", "arbitrary"),
                                             vmem_limit_bytes=VMEM_LIMIT),
        name=name,
    )(qa, ka, va)


def _dil_kernel(q_ref, k_ref, v_ref, bias_ref, o_ref, acc_ref, m_ref, *, t):
    span = DIL_SPAN
    lane_q = lax.broadcasted_iota(jnp.int32, (span, LANES), 1)
    c_hd = HEAD_DIM ** -0.5 * LOG2E

    def rows(ref_get, start, size, dil):
        if dil == 1:
            return ref_get(pl.ds(start, size))
        return ref_get(pl.ds(start, size, stride=dil))

    for hh in range(2):
        own_q = (lane_q < HEAD_DIM) if hh == 0 else (lane_q >= HEAD_DIM)

        for br, (window, dil) in enumerate(DIL_PAIRS):
            assert window // dil == span
            nb = t // dil // span

            def block(r, n, first, br=br, dil=dil, own_q=own_q, hh=hh):
                q0 = r + n * span * dil
                q2 = rows(lambda s: q_ref[0, s, :], q0, span, dil)
                qz = (jnp.where(own_q, q2, 0.0) * c_hd).astype(BF16)
                if first:
                    k0, nkeys = q0, span
                    bias = bias_ref[hh, br, :, span:]
                else:
                    k0, nkeys = q0 - span * dil, 2 * span
                    bias = bias_ref[hh, br]
                k2 = rows(lambda s: k_ref[0, s, :], k0, nkeys, dil).astype(BF16)
                v2 = rows(lambda s: v_ref[0, s, :], k0, nkeys, dil)
                lane_v = lax.broadcasted_iota(jnp.int32, (nkeys, LANES), 1)
                own_v = (lane_v < HEAD_DIM) if hh == 0 else (lane_v >= HEAD_DIM)
                va = jnp.where(own_v, v2, 1.0).astype(BF16)
                s = _dot_nt(qz, k2) + bias
                m = jnp.max(s, axis=1, keepdims=True)
                p = jnp.exp2(s - m)
                oa = _dot(p.astype(BF16), va)
                mb = jnp.broadcast_to(m, (span, LANES))
                if br == 0:
                    acc_ref[hh, pl.ds(q0, span), :] = oa
                    m_ref[hh, pl.ds(q0, span), :] = mb
                else:
                    sl = pl.ds(q0, span, stride=dil)
                    m_old = m_ref[hh, sl, :]
                    m_new = jnp.maximum(m_old, mb)
                    acc_ref[hh, sl, :] = (acc_ref[hh, sl, :] * jnp.exp2(m_old - m_new)
                                          + oa * jnp.exp2(mb - m_new))
                    m_ref[hh, sl, :] = m_new

            def residue(r, carry, nb=nb, block=block):
                block(r, 0, True)

                def inner(n, c):
                    block(r, n, False)
                    return c

                lax.fori_loop(1, nb, inner, 0)
                return carry

            lax.fori_loop(0, dil, residue, 0)

    chunk = 512
    lane_c = lax.broadcasted_iota(jnp.int32, (chunk, LANES), 1)

    def finish(ci, carry):
        sl = pl.ds(ci * chunk, chunk)
        a0 = acc_ref[0, sl, :]
        a1 = acc_ref[1, sl, :]
        o_ref[0, sl, :] = jnp.where(lane_c < HEAD_DIM, a0 / a0[:, HEAD_DIM:HEAD_DIM + 1],
                                    a1 / a1[:, 0:1]).astype(BF16)
        return carry

    lax.fori_loop(0, t // chunk, finish, 0)


def _dil_bias_table():
    qi = np.arange(DIL_SPAN)[:, None]
    kj = np.arange(2 * DIL_SPAN)[None, :]
    delta = DIL_SPAN + qi - kj
    valid = (delta >= 0) & (delta <= DIL_SPAN)
    tab = np.zeros((N_GROUP, len(DIL_PAIRS), DIL_SPAN, 2 * DIL_SPAN), np.float32)
    for hd in range(N_GROUP):
        for br, (_, dil) in enumerate(DIL_PAIRS):
            tab[hd, br] = np.where(valid, -SLOPES_DIL[hd] * LOG2E * (delta * dil), NEG)
    return tab


def _dil_call(yd, bias):
    b, t, _ = yd.shape
    for window, dil in DIL_PAIRS:
        assert (t // dil) % DIL_SPAN == 0 and window // dil == DIL_SPAN
    n_pair = N_GROUP // 2
    return pl.pallas_call(
        functools.partial(_dil_kernel, t=t),
        out_shape=jax.ShapeDtypeStruct((b, t, HW), BF16),
        grid=(b, n_pair),
        in_specs=[
            pl.BlockSpec((1, t, LANES), lambda bi, p: (bi, 0, p)),
            pl.BlockSpec((1, t, LANES), lambda bi, p: (bi, 0, n_pair + p)),
            pl.BlockSpec((1, t, LANES), lambda bi, p: (bi, 0, 2 * n_pair + p)),
            pl.BlockSpec((2, len(DIL_PAIRS), DIL_SPAN, 2 * DIL_SPAN), lambda bi, p: (p, 0, 0, 0)),
        ],
        out_specs=pl.BlockSpec((1, t, LANES), lambda bi, p: (bi, 0, p)),
        scratch_shapes=[pltpu.VMEM((2, t, LANES), F32), pltpu.VMEM((2, t, LANES), F32)],
        compiler_params=pltpu.CompilerParams(dimension_semantics=("arbitrary", "arbitrary"),
                                             vmem_limit_bytes=VMEM_LIMIT),
        name="dilated",
    )(yd, yd, yd, bias)


def _gather_cols(w, idx, sign=None):
    wp = jnp.concatenate([w, jnp.zeros((w.shape[0], 1), w.dtype)], axis=1)
    idx = np.where(idx < 0, w.shape[1], idx)
    out = jnp.take(wp, jnp.asarray(idx, jnp.int32), axis=1)
    if sign is not None:
        out = out * jnp.asarray(sign, w.dtype)[None, :]
    return out


def _head_cols(base, hd, width=HEAD_DIM):
    return np.arange(base + hd * width, base + (hd + 1) * width)


def _pad(n):
    return np.full((n,), -1, np.int64)


def _index_maps():
    nn, nt = [], []
    nt_sign = []
    for hd in range(N_GROUP):
        nn += [_head_cols(AQ0, hd), _pad(LANES - HEAD_DIM)]
    for hd in range(N_GROUP):
        nn += ([_head_cols(AV0, hd), _pad(HEAD_DIM)] if hd % 2 == 0 else [_pad(HEAD_DIM), _head_cols(AV0, hd)])
    for hd in range(N_GROUP):
        nn += [_head_cols(BQ0, hd), _pad(PIECE_Q0 - HEAD_DIM), np.full((N_PIECE,), BF0 + hd),
               _pad(LANES - PIECE_Q0 - N_PIECE)]
    for hd in range(N_GROUP):
        nn += ([_head_cols(BV0, hd), _pad(HEAD_DIM)] if hd % 2 == 0 else [_pad(HEAD_DIM), _head_cols(BV0, hd)])
    nn += [np.arange(DQ0, DQ0 + 3 * HW), np.arange(CQ0, CQ0 + MLA_Q_RANK), np.arange(CKV0, CKV0 + MLA_KV_RANK)]
    for hd in range(N_GROUP):
        nt += [_head_cols(AK0, hd), _pad(LANES - HEAD_DIM)]
    for hd in range(N_GROUP):
        nt += [_head_cols(BK0, hd), _pad(ONE_Q0 - HEAD_DIM), np.full((N_PIECE,), BF0 + hd),
               _pad(LANES - ONE_Q0 - N_PIECE)]
    half = MLA_ROPE // 2
    nt += [_pad(HEAD_DIM), np.arange(CKR0, CKR0 + MLA_ROPE), _pad(LANES - HEAD_DIM - MLA_ROPE)]
    nt += [_pad(HEAD_DIM), np.arange(CKR0 + half, CKR0 + MLA_ROPE), np.arange(CKR0, CKR0 + half),
           _pad(LANES - HEAD_DIM - MLA_ROPE)]
    nn, nt = np.concatenate(nn), np.concatenate(nt)
    nt_sign = np.ones((NT_END,), np.float32)
    nt_sign[NT_KRH + HEAD_DIM:NT_KRH + HEAD_DIM + half] = -1.0
    assert nn.shape == (NN_END,) and nt.shape == (NT_END,)

    qw = HEAD_DIM + MLA_ROPE
    uq, uqrh, uq_sign = [], [], []
    for hd in range(N_GROUP):
        uq += [np.arange(hd * qw, (hd + 1) * qw), _pad(LANES - qw)]
        r0 = hd * qw + HEAD_DIM
        uqrh += [_pad(HEAD_DIM), np.arange(r0 + half, r0 + MLA_ROPE), np.arange(r0, r0 + half), _pad(LANES - qw)]
        sg = np.ones((LANES,), np.float32)
        sg[HEAD_DIM:HEAD_DIM + half] = -1.0
        uq_sign.append(sg)
    kn, vv = [], []
    for hd in range(N_GROUP):
        kn += [np.arange(hd * LANES, hd * LANES + HEAD_DIM), _pad(HEAD_DIM)]
        vcols = np.arange(hd * LANES + HEAD_DIM, (hd + 1) * LANES)
        vv += ([vcols, _pad(HEAD_DIM)] if hd % 2 == 0 else [_pad(HEAD_DIM), vcols])
    return (nn, nt, nt_sign, np.concatenate(uq), np.concatenate(uqrh), np.concatenate(uq_sign),
            np.concatenate(kn), np.concatenate(vv))


_IDX = _index_maps()


def _rope_tables(t):
    inv = ROPE_THETA ** (-jnp.arange(0, MLA_ROPE, 2, dtype=F32) / MLA_ROPE)
    ang = jnp.arange(t, dtype=F32)[:, None] * inv[None, :]
    cos, sin = jnp.cos(ang), jnp.sin(ang)
    z_lo = jnp.zeros((t, HEAD_DIM), F32)
    z_hi = jnp.zeros((t, LANES - HEAD_DIM - MLA_ROPE), F32)
    cosq = jnp.concatenate([jnp.ones((t, HEAD_DIM), F32), cos, cos, z_hi], axis=1)
    sinq = jnp.concatenate([z_lo, sin, sin, z_hi], axis=1)
    cosk = jnp.concatenate([z_lo, cos, cos, z_hi], axis=1)
    return cosq, sinq, cosk.T, sinq.T


def kernel(x, norm_ffn1, ffn1_gate, ffn1_up, ffn1_down, norm_mix, w_in, forget_bias, mla_q_norm,
           mla_kv_norm, w_uq, w_ukv, w_out, norm_ffn2, ffn2_gate, ffn2_up, ffn2_down, norm_final):
    b, t, d = x.shape
    depth = w_in.shape[0]
    assert d == D_MODEL and t % TILE == 0 and (b * t) % FFN_TILE == 0
    nn_idx, nt_idx, nt_sign, uq_idx, uqrh_idx, uq_sign, kn_idx, vv_idx = _IDX
    cosq, sinq, cosk, sink = _rope_tables(t)
    tri = jnp.tril(jnp.ones((TILE, TILE), F32)).astype(BF16)
    trit = tri.T
    dil_bias = jnp.asarray(_dil_bias_table())
    lane = np.arange(LANES)
    fq_mask = jnp.asarray(((lane >= PIECE_Q0) & (lane < PIECE_Q0 + N_PIECE)).astype(np.float32))
    fk_mask = jnp.asarray(((lane >= ONE_Q0) & (lane < ONE_Q0 + N_PIECE)).astype(np.float32))

    xf = x.reshape(b * t, d)
    for l in range(depth):
        bf = lambda w: w.astype(BF16)
        row = lambda v: v.reshape(1, -1)
        mix_prev = None
        xf = _ffn_call(xf, None, None, row(norm_ffn1[l]), bf(ffn1_gate[l]), bf(ffn1_up[l]),
                       bf(ffn1_down[l]), None)
        wnn = bf(_gather_cols(w_in[l], nn_idx))
        wnt = bf(_gather_cols(w_in[l], nt_idx, nt_sign).T)
        fbq = forget_bias[l][:, None, None] * fq_mask[None, None, :]
        fbk = forget_bias[l][:, None, None] * fk_mask[None, :, None]
        wuq = bf(_gather_cols(w_uq[l], uq_idx))
        wuqrh = bf(_gather_cols(w_uq[l], uqrh_idx, uq_sign))
        wknt = bf(_gather_cols(w_ukv[l], kn_idx).T)
        wv = bf(_gather_cols(w_ukv[l], vv_idx))
        (qa_a0, ka_a, va_a, qa_b, ka_b, va_b, qa_c, ka_c, va_c, yd) = _proj_call(
            xf.reshape(b, t, d), row(norm_mix[l]), wnn, wnt, fbq, fbk, tri, trit,
            row(mla_q_norm[l]), row(mla_kv_norm[l]), wuq, wuqrh, wknt, wv, cosq, sinq, cosk, sink)
        qa_a = _gate_call(qa_a0, ka_a)
        o_a = _attn_call(qa_a, ka_a, va_a, "attn_moba")
        o_b = _attn_call(qa_b, ka_b, va_b, "attn_forget")
        o_c = _attn_call(qa_c, ka_c, va_c, "attn_mla")
        o_d = _dil_call(yd, dil_bias)
        mix = [o.reshape(b * t, HW) for o in (o_a, o_b, o_c, o_d)]
        xf = _ffn_call(xf, mix, bf(w_out[l]), row(norm_ffn2[l]), bf(ffn2_gate[l]), bf(ffn2_up[l]),
                       bf(ffn2_down[l]), row(norm_final) if l == depth - 1 else None)
    return xf.reshape(b, t, d)
```

```python
import functools

import numpy as np
import jax
import jax.numpy as jnp
from jax import lax
from jax.experimental import pallas as pl
from jax.experimental.pallas import tpu as pltpu

F32 = jnp.float32
BF16 = jnp.bfloat16

D_MODEL = 1024
HEAD_DIM = 64
N_GROUP = 4
HW = N_GROUP * HEAD_DIM
LANES = 128
MOBA_BLOCK = 256
MOBA_TOPK = 3
MLA_Q_RANK = 256
MLA_KV_RANK = 128
MLA_ROPE = 32
ROPE_THETA = 10000.0
DIL_PAIRS = ((128, 1), (512, 4), (2048, 16))
DIL_SPAN = 128
D_FF = 2816
RMS_EPS = 1e-6
NEG = -1e30
LOG2E = 1.4426950408889634

SEL0 = 64
ONE_Q0 = 80
PIECE_Q0 = 83
N_PIECE = 3

TILE = 512
FFN_TILE = 512
FF_CHUNK = 256
VMEM_LIMIT = 56 * 1024 * 1024

_ALIBI = [2.0 ** (-8.0 * (i + 1) / 8) for i in range(8)]
SLOPES_MOBA = _ALIBI[1::2]
SLOPES_DIL = _ALIBI[0::2]

_WIDTHS = (HW, HW, HW, HW, HW, HW, N_GROUP, MLA_Q_RANK, MLA_KV_RANK, MLA_ROPE, HW, HW, HW)
_OFF = np.concatenate([[0], np.cumsum(_WIDTHS)]).astype(np.int64)
(AQ0, AK0, AV0, BQ0, BK0, BV0, BF0, CQ0, CKV0, CKR0, DQ0, DK0, DV0, IN_WIDTH) = [int(v) for v in _OFF]


def _dot(a, b):
    return jnp.dot(a, b, preferred_element_type=F32)


def _dot_nt(a, b):
    return lax.dot_general(a, b, (((1,), (1,)), ((), ())), preferred_element_type=F32)


def _rms(x, g):
    return x * lax.rsqrt(jnp.mean(x * x, axis=-1, keepdims=True) + RMS_EPS) * g


def _split3(x):
    p0 = x.astype(BF16)
    r = x - p0.astype(F32)
    p1 = r.astype(BF16)
    p2 = (r - p1.astype(F32)).astype(BF16)
    return p0, p1, p2


def _log_sigmoid(x):
    return jnp.minimum(x, 0.0) - jnp.log(1.0 + jnp.exp(-jnp.abs(x)))


def _ffn_kernel(*refs, has_mix, final_norm):
    it = iter(refs)
    x_ref = next(it)
    if has_mix:
        o_refs = [next(it) for _ in range(4)]
        wo_ref = next(it)
    g_ref, wg_ref, wu_ref, wd_ref = next(it), next(it), next(it), next(it)
    gf_ref = next(it) if final_norm else None
    out_ref = next(it)
    act_ref = next(it)

    x = x_ref[...]
    if has_mix:
        for i, o_ref in enumerate(o_refs):
            x = x + _dot(o_ref[...], wo_ref[i * HW:(i + 1) * HW, :])
    h = _rms(x, g_ref[...]).astype(BF16)
    for c in range(D_FF // FF_CHUNK):
        sl = slice(c * FF_CHUNK, (c + 1) * FF_CHUNK)
        gate = _dot(h, wg_ref[:, sl])
        up = _dot(h, wu_ref[:, sl])
        act_ref[:, sl] = (gate * jax.nn.sigmoid(gate) * up).astype(BF16)
    x = x + 0.5 * _dot(act_ref[...], wd_ref[...])
    if final_norm:
        x = _rms(x, gf_ref[...])
    out_ref[...] = x


def _const_spec(shape):
    nd = len(shape)
    return pl.BlockSpec(shape, lambda *_: (0,) * nd, pipeline_mode=pl.Buffered(1))


def _ffn_call(x, mix, w_out, g, wg, wu, wd, g_final):
    n = x.shape[0]
    tm = FFN_TILE
    has_mix = mix is not None
    final_norm = g_final is not None
    row = lambda i: (i, 0)
    args, specs = [x], [pl.BlockSpec((tm, D_MODEL), row)]
    if has_mix:
        for o in mix:
            args.append(o)
            specs.append(pl.BlockSpec((tm, HW), row))
        args.append(w_out)
        specs.append(_const_spec((D_MODEL, D_MODEL)))
    args += [g, wg, wu, wd]
    specs += [_const_spec((1, D_MODEL)), _const_spec((D_MODEL, D_FF)), _const_spec((D_MODEL, D_FF)),
              _const_spec((D_FF, D_MODEL))]
    if final_norm:
        args.append(g_final)
        specs.append(_const_spec((1, D_MODEL)))
    return pl.pallas_call(
        functools.partial(_ffn_kernel, has_mix=has_mix, final_norm=final_norm),
        out_shape=jax.ShapeDtypeStruct((n, D_MODEL), F32),
        grid=(n // tm,),
        in_specs=specs,
        out_specs=pl.BlockSpec((tm, D_MODEL), row),
        scratch_shapes=[pltpu.VMEM((tm, D_FF), BF16)],
        compiler_params=pltpu.CompilerParams(dimension_semantics=("arbitrary",),
                                             vmem_limit_bytes=VMEM_LIMIT),
        name="ffn_mix" if has_mix else "ffn",
    )(*args)


NN_AQ, NN_AV, NN_BQ, NN_BV, NN_D, NN_CQ, NN_CKV, NN_F, NN_END = 0, 512, 1024, 1536, 2048, 2816, 3072, 3200, 3328
NT_AK, NT_BK, NT_KR, NT_KRH, NT_END = 0, 512, 1024, 1152, 1280

FG_A0 = 64
FG_B0 = FG_A0 + N_GROUP * N_PIECE
FG_END = FG_B0 + N_GROUP * N_PIECE


def _pieces_by(which, val):
    p0, p1, p2 = _split3(val)
    return jnp.where(which == 0, p0.astype(F32), jnp.where(which == 1, p1.astype(F32), p2.astype(F32)))


def _pieces_select(idx, base, val):
    p0, p1, p2 = _split3(val)
    return jnp.where(idx == base, p0.astype(F32),
                     jnp.where(idx == base + 1, p1.astype(F32), p2.astype(F32)))


def _with_ones(v, head):
    lane = lax.broadcasted_iota(jnp.int32, v.shape, 1)
    keep = (lane < HEAD_DIM) if head % 2 == 0 else (lane >= HEAD_DIM)
    return jnp.where(keep, v, 1.0).astype(BF16)


def _proj_kernel(x_ref, g_ref, wnn_ref, wnt_ref, fb_ref, pq_ref, pk_ref, tri_ref,
                 qn_ref, kvn_ref, wuq_ref, wuqrh_ref, wknt_ref, wv_ref,
                 cosq_ref, sinq_ref, cosk_ref, sink_ref,
                 qa_a, ka_a, va_a, qa_b, ka_b, va_b, qa_c, ka_c, va_c, yd_ref,
                 carry, *, tm):
    i = pl.program_id(1)
    h = _rms(x_ref[0], g_ref[...]).astype(BF16)
    lane = lax.broadcasted_iota(jnp.int32, (tm, LANES), 1)
    rowi = lax.broadcasted_iota(jnp.int32, (LANES, tm), 0)
    tok_q = (i * tm + lax.broadcasted_iota(jnp.int32, (tm, LANES), 0)).astype(F32)
    tok_k_i = i * tm + lax.broadcasted_iota(jnp.int32, (LANES, tm), 1)
    tok_k = tok_k_i.astype(F32)
    c_hd = HEAD_DIM ** -0.5 * LOG2E
    head_sl = lambda hd: slice(hd * LANES, (hd + 1) * LANES)

    @pl.when(i == 0)
    def _():
        carry[...] = jnp.zeros_like(carry)

    ones_q = (lane >= ONE_Q0) & (lane < ONE_Q0 + N_PIECE)
    piece_q = (lane >= PIECE_Q0) & (lane < PIECE_Q0 + N_PIECE)
    piece_k = (rowi >= ONE_Q0) & (rowi < ONE_Q0 + N_PIECE)
    ones_k = (rowi >= PIECE_Q0) & (rowi < PIECE_Q0 + N_PIECE)
    zq = _dot(h, wnn_ref[:, NN_AQ:NN_AV])
    zv = _dot(h, wnn_ref[:, NN_AV:NN_BQ])
    zk = _dot_nt(wnt_ref[NT_AK:NT_BK, :], h)
    ind = (rowi >= SEL0) & (rowi < ONE_Q0) & ((tok_k_i // MOBA_BLOCK) == rowi - SEL0)
    for hd in range(N_GROUP):
        slope2 = SLOPES_MOBA[hd] * LOG2E
        ext = _pieces_select(lane, PIECE_Q0, tok_q * (-slope2))
        qa_a[0, hd] = jnp.where(lane < HEAD_DIM, zq[:, head_sl(hd)] * c_hd,
                                jnp.where(ones_q, 1.0, jnp.where(piece_q, ext, 0.0))).astype(BF16)
        ext = _pieces_select(rowi, ONE_Q0, tok_k * slope2)
        ka_a[0, hd, 0] = jnp.where(rowi < HEAD_DIM, zk[head_sl(hd), :],
                                   jnp.where(piece_k, ext,
                                             jnp.where(ones_k | ind, 1.0, 0.0))).astype(BF16)
        va_a[0, hd] = _with_ones(zv[:, head_sl(hd)], hd)

    zcf = _dot(h, wnn_ref[:, NN_CKV:NN_END])
    in_f = (lane >= FG_A0) & (lane < FG_END)
    lf = jnp.where(in_f, _log_sigmoid(zcf[:, LANES:] + fb_ref[...]), 0.0)
    p0, p1, p2 = _split3(lf)
    tri = tri_ref[...]
    dec = _dot(tri, p0) + _dot(tri, p1) + _dot(tri, p2) + carry[0:1, :]
    carry[0:1, :] = dec[tm - 1:tm, :]
    e_q = _pieces_by(pq_ref[...], dec * LOG2E)
    e_k = _pieces_by(pk_ref[...], dec.T * (-LOG2E))
    all_q = (lane >= FG_B0) & (lane < FG_END)
    all_k = (rowi >= FG_A0) & (rowi < FG_B0)
    zq = _dot(h, wnn_ref[:, NN_BQ:NN_BV])
    zv = _dot(h, wnn_ref[:, NN_BV:NN_D])
    zk = _dot_nt(wnt_ref[NT_BK:NT_KR, :], h)
    for hd in range(N_GROUP):
        own_q = (lane >= FG_A0 + N_PIECE * hd) & (lane < FG_A0 + N_PIECE * (hd + 1))
        own_k = (rowi >= FG_B0 + N_PIECE * hd) & (rowi < FG_B0 + N_PIECE * (hd + 1))
        qa_b[0, hd] = jnp.where(lane < HEAD_DIM, zq[:, head_sl(hd)] * c_hd,
                                jnp.where(all_q, e_q, jnp.where(own_q, 1.0, 0.0))).astype(BF16)
        ka_b[0, hd, 0] = jnp.where(rowi < HEAD_DIM, zk[head_sl(hd), :],
                                   jnp.where(all_k, e_k, jnp.where(own_k, 1.0, 0.0))).astype(BF16)
        va_b[0, hd] = _with_ones(zv[:, head_sl(hd)], hd)

    c_mla = (HEAD_DIM + MLA_ROPE) ** -0.5 * LOG2E
    cqn = _rms(_dot(h, wnn_ref[:, NN_CQ:NN_CKV]), qn_ref[...]).astype(BF16)
    ckvn = _rms(zcf[:, :LANES], kvn_ref[...]).astype(BF16)
    cosq, sinq = cosq_ref[...], sinq_ref[...]
    krt = _dot_nt(wnt_ref[NT_KR:NT_KRH, :], h)
    krht = _dot_nt(wnt_ref[NT_KRH:NT_END, :], h)
    rot_k = krt * cosk_ref[...] + krht * sink_ref[...]
    q = _dot(cqn, wuq_ref[...])
    qrh = _dot(cqn, wuqrh_ref[...])
    knt = _dot_nt(wknt_ref[...], ckvn)
    v = _dot(ckvn, wv_ref[...])
    for hd in range(N_GROUP):
        qa_c[0, hd] = ((q[:, head_sl(hd)] * cosq + qrh[:, head_sl(hd)] * sinq) * c_mla).astype(BF16)
        ka_c[0, hd, 0] = (knt[head_sl(hd), :] + rot_k).astype(BF16)
        va_c[0, hd] = _with_ones(v[:, head_sl(hd)], hd)

    yd_ref[0] = _dot(h, wnn_ref[:, NN_D:NN_CQ])


def _proj_call(x3, g, wnn, wnt, fb, pq, pk, tri, qn, kvn, wuq, wuqrh, wknt, wv,
               cosq, sinq, cosk, sink):
    b, t, _ = x3.shape
    tm = TILE
    nk = t // tm
    qa_shape = jax.ShapeDtypeStruct((b, N_GROUP, t, LANES), BF16)
    ka_shape = jax.ShapeDtypeStruct((b, N_GROUP, nk, LANES, tm), BF16)
    qa_spec = pl.BlockSpec((1, N_GROUP, tm, LANES), lambda bi, i: (bi, 0, i, 0))
    ka_spec = pl.BlockSpec((1, N_GROUP, 1, LANES, tm), lambda bi, i: (bi, 0, i, 0, 0))
    in_specs = [
        pl.BlockSpec((1, tm, D_MODEL), lambda bi, i: (bi, i, 0)),
        _const_spec((1, D_MODEL)),
        _const_spec((D_MODEL, NN_END)),
        _const_spec((NT_END, D_MODEL)),
        _const_spec((1, LANES)),
        _const_spec((1, LANES)),
        _const_spec((LANES, 1)),
        _const_spec((tm, tm)),
        _const_spec((1, MLA_Q_RANK)),
        _const_spec((1, MLA_KV_RANK)),
        _const_spec((MLA_Q_RANK, N_GROUP * LANES)),
        _const_spec((MLA_Q_RANK, N_GROUP * LANES)),
        _const_spec((N_GROUP * LANES, MLA_KV_RANK)),
        _const_spec((MLA_KV_RANK, N_GROUP * LANES)),
        pl.BlockSpec((tm, LANES), lambda bi, i: (i, 0)),
        pl.BlockSpec((tm, LANES), lambda bi, i: (i, 0)),
        pl.BlockSpec((LANES, tm), lambda bi, i: (0, i)),
        pl.BlockSpec((LANES, tm), lambda bi, i: (0, i)),
    ]
    out_shape = [qa_shape, ka_shape, qa_shape] * 3 + [jax.ShapeDtypeStruct((b, t, 3 * HW), F32)]
    out_specs = [qa_spec, ka_spec, qa_spec] * 3 + [pl.BlockSpec((1, tm, 3 * HW), lambda bi, i: (bi, i, 0))]
    return pl.pallas_call(
        functools.partial(_proj_kernel, tm=tm),
        out_shape=out_shape,
        grid=(b, nk),
        in_specs=in_specs,
        out_specs=out_specs,
        scratch_shapes=[pltpu.VMEM((8, LANES), F32)],
        compiler_params=pltpu.CompilerParams(dimension_semantics=("arbitrary", "arbitrary"),
                                             vmem_limit_bytes=VMEM_LIMIT),
        name="proj",
    )(x3, g, wnn, wnt, fb, pq, pk, tri, qn, kvn, wuq, wuqrh, wknt, wv, cosq, sinq, cosk, sink)


def _gate_kernel(qa0_ref, ka_ref, qa_ref, *, t, tk):
    nk = t // tk
    nblk = t // MOBA_BLOCK
    blk = lax.broadcasted_iota(jnp.int32, (nblk, t), 0)
    cur = lax.broadcasted_iota(jnp.int32, (nblk, t), 1) // MOBA_BLOCK
    valid = blk < cur
    for hd in range(N_GROUP):
        ksum = jnp.zeros((nblk, LANES), F32)
        for j in range(nk):
            key_blk = (j * tk + lax.broadcasted_iota(jnp.int32, (nblk, tk), 1)) // MOBA_BLOCK
            ind = jnp.where(key_blk == lax.broadcasted_iota(jnp.int32, (nblk, tk), 0), 1.0, 0.0)
            ksum = ksum + _dot_nt(ind.astype(BF16), ka_ref[0, hd, j])
        lane = lax.broadcasted_iota(jnp.int32, (nblk, LANES), 1)
        kmean = jnp.where(lane < HEAD_DIM, ksum * (1.0 / MOBA_BLOCK), 0.0).astype(BF16)
        q0 = qa0_ref[0, hd]
        gate = jnp.where(valid, _dot_nt(kmean, q0), -jnp.inf)
        rank = jnp.zeros((nblk, t), F32)
        for m in range(nblk):
            gm = gate[m:m + 1, :]
            ahead = (gm > gate) | ((gm == gate) & (blk > m))
            rank = rank + jnp.where(ahead, 1.0, 0.0)
        keep = (valid & (rank < MOBA_TOPK)) | (blk == cur)
        selb = jnp.where(keep, 0.0, NEG)
        full = jnp.concatenate([jnp.zeros((SEL0, t), F32), selb,
                                jnp.zeros((LANES - SEL0 - nblk, t), F32)], axis=0)
        qa_ref[0, hd] = (q0.astype(F32) + full.T).astype(BF16)


def _gate_call(qa0, ka):
    b, _, t, _ = qa0.shape
    tk = ka.shape[-1]
    assert (t // MOBA_BLOCK) % 8 == 0 and t // MOBA_BLOCK <= ONE_Q0 - SEL0
    qa_spec = pl.BlockSpec((1, N_GROUP, t, LANES), lambda bi: (bi, 0, 0, 0))
    ka_spec = pl.BlockSpec((1, N_GROUP, t // tk, LANES, tk), lambda bi: (bi, 0, 0, 0, 0))
    return pl.pallas_call(
        functools.partial(_gate_kernel, t=t, tk=tk),
        out_shape=jax.ShapeDtypeStruct(qa0.shape, BF16),
        grid=(b,),
        in_specs=[qa_spec, ka_spec],
        out_specs=qa_spec,
        compiler_params=pltpu.CompilerParams(dimension_semantics=("arbitrary",),
                                             vmem_limit_bytes=VMEM_LIMIT),
        name="moba_gate",
    )(qa0, ka)


def _attn_kernel(qa_ref, ka_ref, va_ref, o_ref, acc_ref, *, tq):
    qi = pl.program_id(1)
    tk = tq
    row = lax.broadcasted_iota(jnp.int32, (tq, tk), 0)
    col = lax.broadcasted_iota(jnp.int32, (tq, tk), 1)
    lane = lax.broadcasted_iota(jnp.int32, (tq, LANES), 1)
    acc_ref[...] = jnp.zeros(acc_ref.shape, F32)

    def step(j, ms, diag):
        out = []
        for hd in range(N_GROUP):
            s = _dot(qa_ref[0, hd], ka_ref[0, hd, j])
            if diag:
                s = jnp.where(col <= row, s, NEG)
            m_new = jnp.maximum(ms[hd], jnp.max(s, axis=1, keepdims=True))
            p = jnp.exp2(s - m_new)
            pv = _dot(p.astype(BF16), va_ref[0, hd, pl.ds(pl.multiple_of(j * tk, tk), tk), :])
            acc_ref[hd] = jnp.exp2(ms[hd] - m_new) * acc_ref[hd] + pv
            out.append(m_new)
        return tuple(out)

    m0 = tuple(jnp.full((tq, 1), -jnp.inf, F32) for _ in range(N_GROUP))
    ms = lax.fori_loop(0, qi, lambda j, c: step(j, c, False), m0)
    step(qi, ms, True)

    for pair in range(N_GROUP // 2):
        even, odd = acc_ref[2 * pair], acc_ref[2 * pair + 1]
        out = jnp.where(lane < HEAD_DIM, even / even[:, HEAD_DIM:HEAD_DIM + 1], odd / odd[:, 0:1])
        o_ref[0, :, pair * LANES:(pair + 1) * LANES] = out.astype(BF16)


def _attn_call(qa, ka, va, name):
    b, _, t, _ = qa.shape
    tq = ka.shape[-1]
    nk = t // tq
    return pl.pallas_call(
        functools.partial(_attn_kernel, tq=tq),
        out_shape=jax.ShapeDtypeStruct((b, t, HW), BF16),
        grid=(b, nk),
        in_specs=[
            pl.BlockSpec((1, N_GROUP, tq, LANES), lambda bi, i: (bi, 0, i, 0)),
            pl.BlockSpec((1, N_GROUP, nk, LANES, tq), lambda bi, i: (bi, 0, 0, 0, 0)),
            pl.BlockSpec((1, N_GROUP, t, LANES), lambda bi, i: (bi, 0, 0, 0)),
        ],
        out_specs=pl.BlockSpec((1, tq, HW), lambda bi, i: (bi, i, 0)),
        scratch_shapes=[pltpu.VMEM((N_GROUP, tq, LANES), F32)],
        compiler_params=pltpu.CompilerParams(dimension_semantics=("arbitrary", "arbitrary"),
                                             vmem_limit_bytes=VMEM_LIMIT),
        name=name,
    )(qa, ka, va)


DIL_UNROLL = 4
DIL_M_LANE = (LANES - 1, 0)
DIL_L_LANE = (HEAD_DIM, HEAD_DIM - 1)


def _dil_kernel(q_ref, k_ref, v_ref, bias_ref, o_ref, acc_ref, *, t):
    span = DIL_SPAN
    lane_q = lax.broadcasted_iota(jnp.int32, (span, LANES), 1)
    lane_k = lax.broadcasted_iota(jnp.int32, (2 * span, LANES), 1)
    c_hd = HEAD_DIM ** -0.5 * LOG2E

    def rows(start, size, dil):
        return pl.ds(start, size) if dil == 1 else pl.ds(start, size, stride=dil)

    for br, (window, dil) in enumerate(DIL_PAIRS):
        nb = t // dil // span

        def group(it, carry, br=br, dil=dil, nb=nb):
            for u in range(DIL_UNROLL):
                idx = it * DIL_UNROLL + u
                r, n = idx // nb, idx % nb
                q0 = r + n * (span * dil)
                first = (n == 0).astype(jnp.int32)
                k0 = q0 - (1 - first) * (span * dil)
                q2 = q_ref[0, rows(q0, span, dil), :] * c_hd
                k2 = k_ref[0, rows(k0, 2 * span, dil), :].astype(BF16)
                v2 = v_ref[0, rows(k0, 2 * span, dil), :]
                for hh in range(2):
                    own_q = (lane_q < HEAD_DIM) if hh == 0 else (lane_q >= HEAD_DIM)
                    own_k = (lane_k < HEAD_DIM) if hh == 0 else (lane_k >= HEAD_DIM)
                    qz = jnp.where(own_q, q2, 0.0).astype(BF16)
                    va = jnp.where(own_k, v2, 1.0).astype(BF16)
                    s = _dot_nt(qz, k2) + bias_ref[hh, 2 * br + first]
                    m = jnp.max(s, axis=1, keepdims=True)
                    p = jnp.exp2(s - m)
                    oa = _dot(p.astype(BF16), va)
                    acc_ref[br, hh, rows(q0, span, dil), :] = jnp.where(lane_q == DIL_M_LANE[hh], m, oa)
            return carry

        lax.fori_loop(0, dil * nb // DIL_UNROLL, group, 0)

    chunk = 256
    lane_c = lax.broadcasted_iota(jnp.int32, (chunk, LANES), 1)

    def finish(ci, carry):
        sl = pl.ds(ci * chunk, chunk)
        outs = []
        for hh in range(2):
            parts = [acc_ref[br, hh, sl, :] for br in range(len(DIL_PAIRS))]
            ms = [a[:, DIL_M_LANE[hh]:DIL_M_LANE[hh] + 1] for a in parts]
            mx = functools.reduce(jnp.maximum, ms)
            tot = sum(a * jnp.exp2(m - mx) for a, m in zip(parts, ms))
            outs.append(tot / tot[:, DIL_L_LANE[hh]:DIL_L_LANE[hh] + 1])
        o_ref[0, sl, :] = jnp.where(lane_c < HEAD_DIM, outs[0], outs[1]).astype(BF16)
        return carry

    lax.fori_loop(0, t // chunk, finish, 0)


def _dil_bias_table():
    qi = np.arange(DIL_SPAN)[:, None]
    kj = np.arange(2 * DIL_SPAN)[None, :]
    tab = np.zeros((N_GROUP, 2 * len(DIL_PAIRS), DIL_SPAN, 2 * DIL_SPAN), np.float32)
    for variant in range(2):
        delta = qi - kj + (DIL_SPAN if variant == 0 else 0)
        valid = (delta >= 0) & (delta <= DIL_SPAN)
        for hd in range(N_GROUP):
            for br, (_, dil) in enumerate(DIL_PAIRS):
                tab[hd, 2 * br + variant] = np.where(valid, -SLOPES_DIL[hd] * LOG2E * (delta * dil), NEG)
    return tab


def _dil_call(yd, bias):
    b, t, _ = yd.shape
    for window, dil in DIL_PAIRS:
        nb = t // dil // DIL_SPAN
        assert window // dil == DIL_SPAN and nb * dil * DIL_SPAN == t and nb >= 2
        assert (dil * nb) % DIL_UNROLL == 0
    n_pair = N_GROUP // 2
    n_br = len(DIL_PAIRS)
    return pl.pallas_call(
        functools.partial(_dil_kernel, t=t),
        out_shape=jax.ShapeDtypeStruct((b, t, HW), BF16),
        grid=(b, n_pair),
        in_specs=[
            pl.BlockSpec((1, t, LANES), lambda bi, p: (bi, 0, p)),
            pl.BlockSpec((1, t, LANES), lambda bi, p: (bi, 0, n_pair + p)),
            pl.BlockSpec((1, t, LANES), lambda bi, p: (bi, 0, 2 * n_pair + p)),
            pl.BlockSpec((2, 2 * n_br, DIL_SPAN, 2 * DIL_SPAN), lambda bi, p: (p, 0, 0, 0)),
        ],
        out_specs=pl.BlockSpec((1, t, LANES), lambda bi, p: (bi, 0, p)),
        scratch_shapes=[pltpu.VMEM((n_br, 2, t, LANES), F32)],
        compiler_params=pltpu.CompilerParams(dimension_semantics=("arbitrary", "arbitrary"),
                                             vmem_limit_bytes=VMEM_LIMIT),
        name="dilated",
    )(yd, yd, yd, bias)


def _gather_cols(w, idx, sign=None):
    wp = jnp.concatenate([w, jnp.zeros((w.shape[0], 1), w.dtype)], axis=1)
    idx = np.where(idx < 0, w.shape[1], idx)
    out = jnp.take(wp, jnp.asarray(idx, jnp.int32), axis=1)
    if sign is not None:
        out = out * jnp.asarray(sign, w.dtype)[None, :]
    return out


def _head_cols(base, hd, width=HEAD_DIM):
    return np.arange(base + hd * width, base + (hd + 1) * width)


def _pad(n):
    return np.full((n,), -1, np.int64)


def _index_maps():
    nn, nt = [], []
    nt_sign = []
    for hd in range(N_GROUP):
        nn += [_head_cols(AQ0, hd), _pad(LANES - HEAD_DIM)]
    for hd in range(N_GROUP):
        nn += ([_head_cols(AV0, hd), _pad(HEAD_DIM)] if hd % 2 == 0 else [_pad(HEAD_DIM), _head_cols(AV0, hd)])
    for hd in range(N_GROUP):
        nn += [_head_cols(BQ0, hd), _pad(LANES - HEAD_DIM)]
    for hd in range(N_GROUP):
        nn += ([_head_cols(BV0, hd), _pad(HEAD_DIM)] if hd % 2 == 0 else [_pad(HEAD_DIM), _head_cols(BV0, hd)])
    nn += [np.arange(DQ0, DQ0 + 3 * HW), np.arange(CQ0, CQ0 + MLA_Q_RANK), np.arange(CKV0, CKV0 + MLA_KV_RANK)]
    forget = np.repeat(np.arange(BF0, BF0 + N_GROUP), N_PIECE)
    nn += [_pad(FG_A0), forget, forget, _pad(LANES - FG_END)]
    for hd in range(N_GROUP):
        nt += [_head_cols(AK0, hd), _pad(LANES - HEAD_DIM)]
    for hd in range(N_GROUP):
        nt += [_head_cols(BK0, hd), _pad(LANES - HEAD_DIM)]
    half = MLA_ROPE // 2
    nt += [_pad(HEAD_DIM), np.arange(CKR0, CKR0 + MLA_ROPE), _pad(LANES - HEAD_DIM - MLA_ROPE)]
    nt += [_pad(HEAD_DIM), np.arange(CKR0 + half, CKR0 + MLA_ROPE), np.arange(CKR0, CKR0 + half),
           _pad(LANES - HEAD_DIM - MLA_ROPE)]
    nn, nt = np.concatenate(nn), np.concatenate(nt)
    nt_sign = np.ones((NT_END,), np.float32)
    nt_sign[NT_KRH + HEAD_DIM:NT_KRH + HEAD_DIM + half] = -1.0
    assert nn.shape == (NN_END,) and nt.shape == (NT_END,)

    qw = HEAD_DIM + MLA_ROPE
    uq, uqrh, uq_sign = [], [], []
    for hd in range(N_GROUP):
        uq += [np.arange(hd * qw, (hd + 1) * qw), _pad(LANES - qw)]
        r0 = hd * qw + HEAD_DIM
        uqrh += [_pad(HEAD_DIM), np.arange(r0 + half, r0 + MLA_ROPE), np.arange(r0, r0 + half), _pad(LANES - qw)]
        sg = np.ones((LANES,), np.float32)
        sg[HEAD_DIM:HEAD_DIM + half] = -1.0
        uq_sign.append(sg)
    kn, vv = [], []
    for hd in range(N_GROUP):
        kn += [np.arange(hd * LANES, hd * LANES + HEAD_DIM), _pad(HEAD_DIM)]
        vcols = np.arange(hd * LANES + HEAD_DIM, (hd + 1) * LANES)
        vv += ([vcols, _pad(HEAD_DIM)] if hd % 2 == 0 else [_pad(HEAD_DIM), vcols])
    return (nn, nt, nt_sign, np.concatenate(uq), np.concatenate(uqrh), np.concatenate(uq_sign),
            np.concatenate(kn), np.concatenate(vv))


_IDX = _index_maps()


def _rope_tables(t):
    inv = ROPE_THETA ** (-jnp.arange(0, MLA_ROPE, 2, dtype=F32) / MLA_ROPE)
    ang = jnp.arange(t, dtype=F32)[:, None] * inv[None, :]
    cos, sin = jnp.cos(ang), jnp.sin(ang)
    z_lo = jnp.zeros((t, HEAD_DIM), F32)
    z_hi = jnp.zeros((t, LANES - HEAD_DIM - MLA_ROPE), F32)
    cosq = jnp.concatenate([jnp.ones((t, HEAD_DIM), F32), cos, cos, z_hi], axis=1)
    sinq = jnp.concatenate([z_lo, sin, sin, z_hi], axis=1)
    cosk = jnp.concatenate([z_lo, cos, cos, z_hi], axis=1)
    return cosq, sinq, cosk.T, sinq.T


def kernel(x, norm_ffn1, ffn1_gate, ffn1_up, ffn1_down, norm_mix, w_in, forget_bias, mla_q_norm,
           mla_kv_norm, w_uq, w_ukv, w_out, norm_ffn2, ffn2_gate, ffn2_up, ffn2_down, norm_final):
    b, t, d = x.shape
    depth = w_in.shape[0]
    assert d == D_MODEL and t % TILE == 0 and (b * t) % FFN_TILE == 0
    nn_idx, nt_idx, nt_sign, uq_idx, uqrh_idx, uq_sign, kn_idx, vv_idx = _IDX
    cosq, sinq, cosk, sink = _rope_tables(t)
    tri = jnp.tril(jnp.ones((TILE, TILE), F32)).astype(BF16)
    dil_bias = jnp.asarray(_dil_bias_table())
    lane = np.arange(LANES)
    in_f = (lane >= FG_A0) & (lane < FG_END)
    fg_head = np.where(in_f, ((lane - FG_A0) // N_PIECE) % N_GROUP, 0)
    fg_piece = np.where(in_f, (lane - FG_A0) % N_PIECE, -1).astype(np.int32)
    pq, pk = jnp.asarray(fg_piece[None, :]), jnp.asarray(fg_piece[:, None])

    xf = x.reshape(b * t, d)
    for l in range(depth):
        bf = lambda w: w.astype(BF16)
        row = lambda v: v.reshape(1, -1)
        xf = _ffn_call(xf, None, None, row(norm_ffn1[l]), bf(ffn1_gate[l]), bf(ffn1_up[l]),
                       bf(ffn1_down[l]), None)
        wnn = bf(_gather_cols(w_in[l], nn_idx))
        wnt = bf(_gather_cols(w_in[l], nt_idx, nt_sign).T)
        fb = jnp.where(jnp.asarray(in_f), jnp.take(forget_bias[l], jnp.asarray(fg_head)), 0.0)[None, :]
        wuq = bf(_gather_cols(w_uq[l], uq_idx))
        wuqrh = bf(_gather_cols(w_uq[l], uqrh_idx, uq_sign))
        wknt = bf(_gather_cols(w_ukv[l], kn_idx).T)
        wv = bf(_gather_cols(w_ukv[l], vv_idx))
        (qa_a0, ka_a, va_a, qa_b, ka_b, va_b, qa_c, ka_c, va_c, yd) = _proj_call(
            xf.reshape(b, t, d), row(norm_mix[l]), wnn, wnt, fb, pq, pk, tri,
            row(mla_q_norm[l]), row(mla_kv_norm[l]), wuq, wuqrh, wknt, wv, cosq, sinq, cosk, sink)
        qa_a = _gate_call(qa_a0, ka_a)
        o_a = _attn_call(qa_a, ka_a, va_a, "attn_moba")
        o_b = _attn_call(qa_b, ka_b, va_b, "attn_forget")
        o_c = _attn_call(qa_c, ka_c, va_c, "attn_mla")
        o_d = _dil_call(yd, dil_bias)
        mix = [o.reshape(b * t, HW) for o in (o_a, o_b, o_c, o_d)]
        xf = _ffn_call(xf, mix, bf(w_out[l]), row(norm_ffn2[l]), bf(ffn2_gate[l]), bf(ffn2_up[l]),
                       bf(ffn2_down[l]), row(norm_final) if l == depth - 1 else None)
    return xf.reshape(b, t, d)
```

```python
import functools

import numpy as np
import jax
import jax.numpy as jnp
from jax import lax
from jax.experimental import pallas as pl
from jax.experimental.pallas import tpu as pltpu

F32 = jnp.float32
BF16 = jnp.bfloat16

D_MODEL = 1024
HEAD_DIM = 64
N_GROUP = 4
HW = N_GROUP * HEAD_DIM
LANES = 128
MOBA_BLOCK = 256
MOBA_TOPK = 3
MLA_Q_RANK = 256
MLA_KV_RANK = 128
MLA_ROPE = 32
ROPE_THETA = 10000.0
DIL_PAIRS = ((128, 1), (512, 4), (2048, 16))
DIL_SPAN = 128
D_FF = 2816
RMS_EPS = 1e-6
NEG = -1e30
LOG2E = 1.4426950408889634

N_PIECE = 3
SEL0 = 64
ONE_Q0 = 80
PIECE_Q0 = 83
FG_A0 = 64
FG_B0 = FG_A0 + N_GROUP * N_PIECE
FG_END = FG_B0 + N_GROUP * N_PIECE

TILE = 512
FFN_TILE = 512
FF_CHUNK = 256
VMEM_LIMIT = 56 * 1024 * 1024

_ALIBI = [2.0 ** (-8.0 * (i + 1) / 8) for i in range(8)]
SLOPES_MOBA = _ALIBI[1::2]
SLOPES_DIL = _ALIBI[0::2]

_WIDTHS = (HW, HW, HW, HW, HW, HW, N_GROUP, MLA_Q_RANK, MLA_KV_RANK, MLA_ROPE, HW, HW, HW)
_OFF = np.concatenate([[0], np.cumsum(_WIDTHS)]).astype(np.int64)
(AQ0, AK0, AV0, BQ0, BK0, BV0, BF0, CQ0, CKV0, CKR0, DQ0, DK0, DV0, IN_WIDTH) = [int(v) for v in _OFF]


def _dot(a, b):
    return jnp.dot(a, b, preferred_element_type=F32)


def _dot_nt(a, b):
    return lax.dot_general(a, b, (((1,), (1,)), ((), ())), preferred_element_type=F32)


def _rms(x, g):
    return x * lax.rsqrt(jnp.mean(x * x, axis=-1, keepdims=True) + RMS_EPS) * g


def _split3(x):
    p0 = x.astype(BF16)
    r = x - p0.astype(F32)
    p1 = r.astype(BF16)
    p2 = (r - p1.astype(F32)).astype(BF16)
    return p0, p1, p2


def _log_sigmoid(x):
    return jnp.minimum(x, 0.0) - jnp.log(1.0 + jnp.exp(-jnp.abs(x)))


def _ffn_kernel(*refs, has_mix, final_norm):
    it = iter(refs)
    x_ref = next(it)
    if has_mix:
        o_refs = [next(it) for _ in range(4)]
        wo_ref = next(it)
    g_ref, wg_ref, wu_ref, wd_ref = next(it), next(it), next(it), next(it)
    gf_ref = next(it) if final_norm else None
    out_ref = next(it)

    x = x_ref[...]
    if has_mix:
        for i, o_ref in enumerate(o_refs):
            x = x + _dot(o_ref[...], wo_ref[i * HW:(i + 1) * HW, :])
    h = _rms(x, g_ref[...]).astype(BF16)
    y = jnp.zeros(x.shape, F32)
    for c in range(D_FF // FF_CHUNK):
        sl = slice(c * FF_CHUNK, (c + 1) * FF_CHUNK)
        gate = _dot(h, wg_ref[:, sl].astype(BF16))
        up = _dot(h, wu_ref[:, sl].astype(BF16))
        act = (gate * jax.nn.sigmoid(gate) * up).astype(BF16)
        y = y + _dot(act, wd_ref[sl, :].astype(BF16))
    x = x + 0.5 * y
    if final_norm:
        x = _rms(x, gf_ref[...])
    out_ref[...] = x


def _const_spec(shape):
    nd = len(shape)
    return pl.BlockSpec(shape, lambda *_: (0,) * nd, pipeline_mode=pl.Buffered(1))


def _ffn_call(x, mix, w_out, g, wg, wu, wd, g_final):
    n = x.shape[0]
    tm = FFN_TILE
    has_mix = mix is not None
    final_norm = g_final is not None
    row = lambda i: (i, 0)
    args, specs = [x], [pl.BlockSpec((tm, D_MODEL), row)]
    if has_mix:
        for o in mix:
            args.append(o)
            specs.append(pl.BlockSpec((tm, HW), row))
        args.append(w_out)
        specs.append(_const_spec((D_MODEL, D_MODEL)))
    args += [g, wg, wu, wd]
    specs += [_const_spec((1, D_MODEL)), _const_spec((D_MODEL, D_FF)), _const_spec((D_MODEL, D_FF)),
              _const_spec((D_FF, D_MODEL))]
    if final_norm:
        args.append(g_final)
        specs.append(_const_spec((1, D_MODEL)))
    return pl.pallas_call(
        functools.partial(_ffn_kernel, has_mix=has_mix, final_norm=final_norm),
        out_shape=jax.ShapeDtypeStruct((n, D_MODEL), F32),
        grid=(n // tm,),
        in_specs=specs,
        out_specs=pl.BlockSpec((tm, D_MODEL), row),
        compiler_params=pltpu.CompilerParams(dimension_semantics=("arbitrary",),
                                             vmem_limit_bytes=VMEM_LIMIT),
        name="ffn_mix" if has_mix else "ffn",
    )(*args)


NN_AK, NN_BK, NN_D, NN_CQ, NN_CKV, NN_F, NN_KR, NN_KRH, NN_END = 0, 512, 1024, 1792, 2048, 2176, 2304, 2432, 2560
NT_AQ, NT_AV, NT_BQ, NT_BV, NT_END = 0, 512, 1024, 1536, 2048


def _pieces_select(idx, base, val):
    p0, p1, p2 = _split3(val)
    return jnp.where(idx == base, p0.astype(F32),
                     jnp.where(idx == base + 1, p1.astype(F32), p2.astype(F32)))


def _pieces_by(which, val):
    p0, p1, p2 = _split3(val)
    return jnp.where(which == 0, p0.astype(F32), jnp.where(which == 1, p1.astype(F32), p2.astype(F32)))


def _vt_with_ones(vt, head):
    row = lax.broadcasted_iota(jnp.int32, vt.shape, 0)
    keep = (row < HEAD_DIM) if head % 2 == 0 else (row >= HEAD_DIM)
    return jnp.where(keep, vt, 1.0).astype(BF16)


def _proj_kernel(x_ref, g_ref, wnn_ref, wnt_ref, fb_ref, pq_ref, pk_ref, tri_ref,
                 qn_ref, kvn_ref, wuqt_ref, wuqrht_ref, wkn_ref, wvt_ref,
                 cosq_ref, sinq_ref, cosk_ref, sink_ref,
                 kr_a, qt_a, vt_a, kr_b, qt_b, vt_b, kr_c, qt_c, vt_c, yd_ref,
                 carry, *, tm):
    i = pl.program_id(1)
    h = _rms(x_ref[0], g_ref[...]).astype(BF16)
    lane = lax.broadcasted_iota(jnp.int32, (tm, LANES), 1)
    rowi = lax.broadcasted_iota(jnp.int32, (LANES, tm), 0)
    tok_r_i = i * tm + lax.broadcasted_iota(jnp.int32, (tm, LANES), 0)
    tok_r = tok_r_i.astype(F32)
    tok_c = (i * tm + lax.broadcasted_iota(jnp.int32, (LANES, tm), 1)).astype(F32)
    c_hd = HEAD_DIM ** -0.5 * LOG2E
    head_sl = lambda hd: slice(hd * LANES, (hd + 1) * LANES)

    @pl.when(i == 0)
    def _():
        carry[...] = jnp.zeros_like(carry)

    piece_k = (lane >= ONE_Q0) & (lane < ONE_Q0 + N_PIECE)
    ones_k = (lane >= PIECE_Q0) & (lane < PIECE_Q0 + N_PIECE)
    ones_q = (rowi >= ONE_Q0) & (rowi < ONE_Q0 + N_PIECE)
    piece_q = (rowi >= PIECE_Q0) & (rowi < PIECE_Q0 + N_PIECE)
    ind = (lane >= SEL0) & (lane < ONE_Q0) & ((tok_r_i // MOBA_BLOCK) == lane - SEL0)
    zk = _dot(h, wnn_ref[:, NN_AK:NN_BK])
    zq = _dot_nt(wnt_ref[NT_AQ:NT_AV, :], h)
    zv = _dot_nt(wnt_ref[NT_AV:NT_BQ, :], h)
    for hd in range(N_GROUP):
        slope2 = SLOPES_MOBA[hd] * LOG2E
        ext = _pieces_select(lane, ONE_Q0, tok_r * slope2)
        kr_a[0, hd] = jnp.where(lane < HEAD_DIM, zk[:, head_sl(hd)],
                                jnp.where(piece_k, ext, jnp.where(ones_k | ind, 1.0, 0.0))).astype(BF16)
        ext = _pieces_select(rowi, PIECE_Q0, tok_c * (-slope2))
        qt_a[0, hd, 0] = jnp.where(rowi < HEAD_DIM, zq[head_sl(hd), :] * c_hd,
                                   jnp.where(ones_q, 1.0, jnp.where(piece_q, ext, 0.0))).astype(BF16)
        vt_a[0, hd, 0] = _vt_with_ones(zv[head_sl(hd), :], hd)

    zcf = _dot(h, wnn_ref[:, NN_CKV:NN_KR])
    in_f = (lane >= FG_A0) & (lane < FG_END)
    lf = jnp.where(in_f, _log_sigmoid(zcf[:, LANES:] + fb_ref[...]), 0.0)
    p0, p1, p2 = _split3(lf)
    tri = tri_ref[...]
    dec = _dot(tri, p0) + _dot(tri, p1) + _dot(tri, p2) + carry[0:1, :]
    carry[0:1, :] = dec[tm - 1:tm, :]
    e_k = _pieces_by(pq_ref[...], dec * (-LOG2E))
    e_q = _pieces_by(pk_ref[...], dec.T * LOG2E)
    all_k = (lane >= FG_A0) & (lane < FG_B0)
    all_q = (rowi >= FG_B0) & (rowi < FG_END)
    zk = _dot(h, wnn_ref[:, NN_BK:NN_D])
    zq = _dot_nt(wnt_ref[NT_BQ:NT_BV, :], h)
    zv = _dot_nt(wnt_ref[NT_BV:NT_END, :], h)
    for hd in range(N_GROUP):
        own_k = (lane >= FG_B0 + N_PIECE * hd) & (lane < FG_B0 + N_PIECE * (hd + 1))
        own_q = (rowi >= FG_A0 + N_PIECE * hd) & (rowi < FG_A0 + N_PIECE * (hd + 1))
        kr_b[0, hd] = jnp.where(lane < HEAD_DIM, zk[:, head_sl(hd)],
                                jnp.where(all_k, e_k, jnp.where(own_k, 1.0, 0.0))).astype(BF16)
        qt_b[0, hd, 0] = jnp.where(rowi < HEAD_DIM, zq[head_sl(hd), :] * c_hd,
                                   jnp.where(all_q, e_q, jnp.where(own_q, 1.0, 0.0))).astype(BF16)
        vt_b[0, hd, 0] = _vt_with_ones(zv[head_sl(hd), :], hd)

    c_mla = (HEAD_DIM + MLA_ROPE) ** -0.5 * LOG2E
    cqn = _rms(_dot(h, wnn_ref[:, NN_CQ:NN_CKV]), qn_ref[...]).astype(BF16)
    ckvn = _rms(zcf[:, :LANES], kvn_ref[...]).astype(BF16)
    zr = _dot(h, wnn_ref[:, NN_KR:NN_END])
    rot_k = zr[:, :LANES] * cosk_ref[...] + zr[:, LANES:] * sink_ref[...]
    kn = _dot(ckvn, wkn_ref[...])
    qt = _dot_nt(wuqt_ref[...], cqn)
    qrht = _dot_nt(wuqrht_ref[...], cqn)
    vt = _dot_nt(wvt_ref[...], ckvn)
    cosq, sinq = cosq_ref[...], sinq_ref[...]
    for hd in range(N_GROUP):
        kr_c[0, hd] = (kn[:, head_sl(hd)] + rot_k).astype(BF16)
        qt_c[0, hd, 0] = ((qt[head_sl(hd), :] * cosq + qrht[head_sl(hd), :] * sinq) * c_mla).astype(BF16)
        vt_c[0, hd, 0] = _vt_with_ones(vt[head_sl(hd), :], hd)

    yd_ref[0] = _dot(h, wnn_ref[:, NN_D:NN_CQ])


def _proj_call(x3, g, wnn, wnt, fb, pq, pk, tri, qn, kvn, wuqt, wuqrht, wkn, wvt, cosq, sinq, cosk, sink):
    b, t, _ = x3.shape
    tm = TILE
    nk = t // tm
    kr_shape = jax.ShapeDtypeStruct((b, N_GROUP, t, LANES), BF16)
    qt_shape = jax.ShapeDtypeStruct((b, N_GROUP, nk, LANES, tm), BF16)
    kr_spec = pl.BlockSpec((1, N_GROUP, tm, LANES), lambda bi, i: (bi, 0, i, 0))
    qt_spec = pl.BlockSpec((1, N_GROUP, 1, LANES, tm), lambda bi, i: (bi, 0, i, 0, 0))
    in_specs = [
        pl.BlockSpec((1, tm, D_MODEL), lambda bi, i: (bi, i, 0)),
        _const_spec((1, D_MODEL)),
        _const_spec((D_MODEL, NN_END)),
        _const_spec((NT_END, D_MODEL)),
        _const_spec((1, LANES)),
        _const_spec((1, LANES)),
        _const_spec((LANES, 1)),
        _const_spec((tm, tm)),
        _const_spec((1, MLA_Q_RANK)),
        _const_spec((1, MLA_KV_RANK)),
        _const_spec((N_GROUP * LANES, MLA_Q_RANK)),
        _const_spec((N_GROUP * LANES, MLA_Q_RANK)),
        _const_spec((MLA_KV_RANK, N_GROUP * LANES)),
        _const_spec((N_GROUP * LANES, MLA_KV_RANK)),
        pl.BlockSpec((LANES, tm), lambda bi, i: (0, i)),
        pl.BlockSpec((LANES, tm), lambda bi, i: (0, i)),
        pl.BlockSpec((tm, LANES), lambda bi, i: (i, 0)),
        pl.BlockSpec((tm, LANES), lambda bi, i: (i, 0)),
    ]
    out_shape = [kr_shape, qt_shape, qt_shape] * 3 + [jax.ShapeDtypeStruct((b, t, 3 * HW), F32)]
    out_specs = [kr_spec, qt_spec, qt_spec] * 3 + [pl.BlockSpec((1, tm, 3 * HW), lambda bi, i: (bi, i, 0))]
    return pl.pallas_call(
        functools.partial(_proj_kernel, tm=tm),
        out_shape=out_shape,
        grid=(b, nk),
        in_specs=in_specs,
        out_specs=out_specs,
        scratch_shapes=[pltpu.VMEM((8, LANES), F32)],
        compiler_params=pltpu.CompilerParams(dimension_semantics=("arbitrary", "arbitrary"),
                                             vmem_limit_bytes=VMEM_LIMIT),
        name="proj",
    )(x3, g, wnn, wnt, fb, pq, pk, tri, qn, kvn, wuqt, wuqrht, wkn, wvt, cosq, sinq, cosk, sink)


def _gate_kernel(qt0_ref, kr_ref, qt_ref, *, t, tq):
    nq = t // tq
    nblk = t // MOBA_BLOCK
    blk = lax.broadcasted_iota(jnp.int32, (nblk, tq), 0)
    lane = lax.broadcasted_iota(jnp.int32, (nblk, LANES), 1)
    for hd in range(N_GROUP):
        ksum = jnp.zeros((nblk, LANES), F32)
        for j in range(nq):
            key_blk = (j * tq + lax.broadcasted_iota(jnp.int32, (nblk, tq), 1)) // MOBA_BLOCK
            ind = jnp.where(key_blk == blk, 1.0, 0.0).astype(BF16)
            ksum = ksum + _dot(ind, kr_ref[0, hd, j * tq:(j + 1) * tq, :])
        kmean = jnp.where(lane < HEAD_DIM, ksum * (1.0 / MOBA_BLOCK), 0.0).astype(BF16)
        for qi in range(nq):
            q0 = qt0_ref[0, hd, qi]
            cur = (qi * tq + lax.broadcasted_iota(jnp.int32, (nblk, tq), 1)) // MOBA_BLOCK
            valid = blk < cur
            gate = jnp.where(valid, _dot(kmean, q0), -jnp.inf)
            rank = jnp.zeros((nblk, tq), F32)
            for m in range(nblk):
                gm = gate[m:m + 1, :]
                ahead = (gm > gate) | ((gm == gate) & (blk > m))
                rank = rank + jnp.where(ahead, 1.0, 0.0)
            keep = (valid & (rank < MOBA_TOPK)) | (blk == cur)
            selb = jnp.where(keep, 0.0, NEG).astype(BF16)
            qt_ref[0, hd, qi] = jnp.concatenate([q0[:SEL0], selb, q0[SEL0 + nblk:]], axis=0)


def _gate_call(qt0, kr):
    b, _, nq, _, tq = qt0.shape
    t = nq * tq
    nblk = t // MOBA_BLOCK
    assert nblk % 16 == 0 and nblk <= ONE_Q0 - SEL0
    qt_spec = pl.BlockSpec((1, N_GROUP, nq, LANES, tq), lambda bi: (bi, 0, 0, 0, 0))
    kr_spec = pl.BlockSpec((1, N_GROUP, t, LANES), lambda bi: (bi, 0, 0, 0))
    return pl.pallas_call(
        functools.partial(_gate_kernel, t=t, tq=tq),
        out_shape=jax.ShapeDtypeStruct(qt0.shape, BF16),
        grid=(b,),
        in_specs=[qt_spec, kr_spec],
        out_specs=qt_spec,
        compiler_params=pltpu.CompilerParams(dimension_semantics=("arbitrary",),
                                             vmem_limit_bytes=VMEM_LIMIT),
        name="moba_gate",
    )(qt0, kr)


def _attn_kernel(kr_ref, qt_ref, vt_ref, o_ref, acc_ref, *, tq):
    qi = pl.program_id(1)
    tk = tq
    keyi = lax.broadcasted_iota(jnp.int32, (tk, tq), 0)
    qryi = lax.broadcasted_iota(jnp.int32, (tk, tq), 1)
    rowi = lax.broadcasted_iota(jnp.int32, (LANES, tq), 0)
    acc_ref[...] = jnp.zeros(acc_ref.shape, F32)

    def step(j, ms, diag):
        k0 = pl.multiple_of(j * tk, tk)
        sts = [_dot(kr_ref[0, hd, pl.ds(k0, tk), :], qt_ref[0, hd, 0]) for hd in range(N_GROUP)]
        out = []
        for hd in range(N_GROUP):
            st = sts[hd]
            if diag:
                st = jnp.where(keyi <= qryi, st, NEG)
            m_new = jnp.maximum(ms[hd], jnp.max(st, axis=0, keepdims=True))
            p = jnp.exp2(st - m_new)
            pv = _dot(vt_ref[0, hd, j], p.astype(BF16))
            acc_ref[hd] = jnp.exp2(ms[hd] - m_new) * acc_ref[hd] + pv
            out.append(m_new)
        return tuple(out)

    m0 = tuple(jnp.full((1, tq), -jnp.inf, F32) for _ in range(N_GROUP))
    ms = lax.fori_loop(0, qi, lambda j, c: step(j, c, False), m0)
    step(qi, ms, True)

    for pair in range(N_GROUP // 2):
        even, odd = acc_ref[2 * pair], acc_ref[2 * pair + 1]
        out_t = jnp.where(rowi < HEAD_DIM, even / even[HEAD_DIM:HEAD_DIM + 1, :], odd / odd[0:1, :])
        o_ref[0, :, pair * LANES:(pair + 1) * LANES] = out_t.T.astype(BF16)


def _attn_call(kr, qt, vt, name):
    b, _, t, _ = kr.shape
    tq = qt.shape[-1]
    nk = t // tq
    return pl.pallas_call(
        functools.partial(_attn_kernel, tq=tq),
        out_shape=jax.ShapeDtypeStruct((b, t, HW), BF16),
        grid=(b, nk),
        in_specs=[
            pl.BlockSpec((1, N_GROUP, t, LANES), lambda bi, i: (bi, 0, 0, 0)),
            pl.BlockSpec((1, N_GROUP, 1, LANES, tq), lambda bi, i: (bi, 0, i, 0, 0)),
            pl.BlockSpec((1, N_GROUP, nk, LANES, tq), lambda bi, i: (bi, 0, 0, 0, 0)),
        ],
        out_specs=pl.BlockSpec((1, tq, HW), lambda bi, i: (bi, i, 0)),
        scratch_shapes=[pltpu.VMEM((N_GROUP, LANES, tq), F32)],
        compiler_params=pltpu.CompilerParams(dimension_semantics=("arbitrary", "arbitrary"),
                                             vmem_limit_bytes=VMEM_LIMIT),
        name=name,
    )(kr, qt, vt)


DIL_UNROLL = 8
DIL_M_LANE = (LANES - 1, 0)
DIL_L_LANE = (HEAD_DIM, HEAD_DIM - 1)


def _dil_kernel(q_ref, k_ref, v_ref, bias_ref, bias2_ref, o_ref, acc_ref, *, t):
    span = DIL_SPAN
    c_hd = HEAD_DIM ** -0.5 * LOG2E

    def rows(start, size, dil):
        return pl.ds(start, size) if dil == 1 else pl.ds(start, size, stride=dil)

    def own(shape, hh):
        lane = lax.broadcasted_iota(jnp.int32, shape, 1)
        return (lane < HEAD_DIM) if hh == 0 else (lane >= HEAD_DIM)

    def finish_blocks(work, br, dil):
        for hh, q0, nq, v2, bias, qk in work:
            va = jnp.where(own(v2.shape, hh), v2, 1.0).astype(BF16)
            s = qk + bias
            m = jnp.max(s, axis=1, keepdims=True)
            p = jnp.exp2(s - m)
            oa = _dot(p.astype(BF16), va)
            lane = lax.broadcasted_iota(jnp.int32, oa.shape, 1)
            acc_ref[br, hh, rows(q0, nq, dil), :] = jnp.where(lane == DIL_M_LANE[hh], m, oa)

    for br, (window, dil) in enumerate(DIL_PAIRS):
        nb = t // dil // span

        def group(it, carry, br=br, dil=dil, nb=nb):
            work = []
            for u in range(DIL_UNROLL):
                idx = it * DIL_UNROLL + u
                r, n = idx // nb, idx % nb
                q0 = r + n * (span * dil)
                first = 1 - jnp.minimum(n, 1)
                k0 = q0 - (1 - first) * (span * dil)
                q2 = q_ref[0, rows(q0, span, dil), :] * c_hd
                k2 = k_ref[0, rows(k0, 2 * span, dil), :].astype(BF16)
                v2 = v_ref[0, rows(k0, 2 * span, dil), :]
                for hh in range(2):
                    qz = jnp.where(own(q2.shape, hh), q2, 0.0).astype(BF16)
                    work.append((hh, q0, span, v2, bias_ref[hh, 2 * br + first], _dot_nt(qz, k2)))
            finish_blocks(work, br, dil)
            return carry

        def group_whole(it, carry, br=br, dil=dil):
            work = []
            for u in range(DIL_UNROLL // 2):
                r = it * (DIL_UNROLL // 2) + u
                q2 = q_ref[0, rows(r, 2 * span, dil), :] * c_hd
                k2 = k_ref[0, rows(r, 2 * span, dil), :].astype(BF16)
                v2 = v_ref[0, rows(r, 2 * span, dil), :]
                for hh in range(2):
                    qz = jnp.where(own(q2.shape, hh), q2, 0.0).astype(BF16)
                    work.append((hh, r, 2 * span, v2, bias2_ref[hh], _dot_nt(qz, k2)))
            finish_blocks(work, br, dil)
            return carry

        if nb == 2:
            lax.fori_loop(0, dil // (DIL_UNROLL // 2), group_whole, 0)
        else:
            lax.fori_loop(0, dil * nb // DIL_UNROLL, group, 0)

    chunk = 256
    lane_c = lax.broadcasted_iota(jnp.int32, (chunk, LANES), 1)

    def finish(ci, carry):
        sl = pl.ds(ci * chunk, chunk)
        outs = []
        for hh in range(2):
            parts = [acc_ref[br, hh, sl, :] for br in range(len(DIL_PAIRS))]
            ms = [a[:, DIL_M_LANE[hh]:DIL_M_LANE[hh] + 1] for a in parts]
            mx = functools.reduce(jnp.maximum, ms)
            tot = sum(a * jnp.exp2(m - mx) for a, m in zip(parts, ms))
            outs.append(tot / tot[:, DIL_L_LANE[hh]:DIL_L_LANE[hh] + 1])
        o_ref[0, sl, :] = jnp.where(lane_c < HEAD_DIM, outs[0], outs[1]).astype(BF16)
        return carry

    lax.fori_loop(0, t // chunk, finish, 0)


def _dil_bias_table():
    qi = np.arange(DIL_SPAN)[:, None]
    kj = np.arange(2 * DIL_SPAN)[None, :]
    tab = np.zeros((N_GROUP, 2 * len(DIL_PAIRS), DIL_SPAN, 2 * DIL_SPAN), np.float32)
    for variant in range(2):
        delta = qi - kj + (DIL_SPAN if variant == 0 else 0)
        valid = (delta >= 0) & (delta <= DIL_SPAN)
        for hd in range(N_GROUP):
            for br, (_, dil) in enumerate(DIL_PAIRS):
                tab[hd, 2 * br + variant] = np.where(valid, -SLOPES_DIL[hd] * LOG2E * (delta * dil), NEG)
    return tab


def _dil_bias_whole(dil):
    delta = np.arange(2 * DIL_SPAN)[:, None] - np.arange(2 * DIL_SPAN)[None, :]
    valid = (delta >= 0) & (delta <= DIL_SPAN)
    return np.stack([np.where(valid, -SLOPES_DIL[hd] * LOG2E * (delta * dil), NEG)
                     for hd in range(N_GROUP)]).astype(np.float32)


def _dil_call(yd, bias, bias2):
    b, t, _ = yd.shape
    for window, dil in DIL_PAIRS:
        nb = t // dil // DIL_SPAN
        assert window // dil == DIL_SPAN and nb * dil * DIL_SPAN == t and nb >= 2
        assert (dil * nb) % DIL_UNROLL == 0
    n_pair = N_GROUP // 2
    n_br = len(DIL_PAIRS)
    return pl.pallas_call(
        functools.partial(_dil_kernel, t=t),
        out_shape=jax.ShapeDtypeStruct((b, t, HW), BF16),
        grid=(b, n_pair),
        in_specs=[
            pl.BlockSpec((1, t, LANES), lambda bi, p: (bi, 0, p)),
            pl.BlockSpec((1, t, LANES), lambda bi, p: (bi, 0, n_pair + p)),
            pl.BlockSpec((1, t, LANES), lambda bi, p: (bi, 0, 2 * n_pair + p)),
            pl.BlockSpec((2, 2 * n_br, DIL_SPAN, 2 * DIL_SPAN), lambda bi, p: (p, 0, 0, 0)),
            pl.BlockSpec((2, 2 * DIL_SPAN, 2 * DIL_SPAN), lambda bi, p: (p, 0, 0)),
        ],
        out_specs=pl.BlockSpec((1, t, LANES), lambda bi, p: (bi, 0, p)),
        scratch_shapes=[pltpu.VMEM((n_br, 2, t, LANES), F32)],
        compiler_params=pltpu.CompilerParams(dimension_semantics=("arbitrary", "arbitrary"),
                                             vmem_limit_bytes=VMEM_LIMIT),
        name="dilated",
    )(yd, yd, yd, bias, bias2)


def _cols(w, spec):
    parts = []
    for item in spec:
        if isinstance(item, int):
            parts.append(jnp.zeros((w.shape[0], item), w.dtype))
        else:
            piece = w[:, item[0]:item[0] + item[1]]
            parts.append(-piece if len(item) == 3 else piece)
    return jnp.concatenate(parts, axis=1)


def _layout_specs():
    half = MLA_ROPE // 2
    pad = LANES - HEAD_DIM
    k_heads = lambda base: [s for hd in range(N_GROUP) for s in ((base + hd * HEAD_DIM, HEAD_DIM), pad)]
    v_heads = lambda base: [s for hd in range(N_GROUP)
                            for s in (((base + hd * HEAD_DIM, HEAD_DIM), pad) if hd % 2 == 0
                                      else (pad, (base + hd * HEAD_DIM, HEAD_DIM)))]
    forget = [(BF0 + hd, 1) for hd in range(N_GROUP) for _ in range(N_PIECE)]
    rope_pad = LANES - HEAD_DIM - MLA_ROPE
    nn = (k_heads(AK0) + k_heads(BK0) + [(DQ0, 3 * HW), (CQ0, MLA_Q_RANK), (CKV0, MLA_KV_RANK)]
          + [FG_A0] + forget + forget + [LANES - FG_END]
          + [HEAD_DIM, (CKR0, MLA_ROPE), rope_pad]
          + [HEAD_DIM, (CKR0 + half, half, -1.0), (CKR0, half), rope_pad])
    nt = k_heads(AQ0) + v_heads(AV0) + k_heads(BQ0) + v_heads(BV0)
    qw = HEAD_DIM + MLA_ROPE
    uq = [s for hd in range(N_GROUP) for s in ((hd * qw, qw), LANES - qw)]
    uqrh = [s for hd in range(N_GROUP)
            for s in (HEAD_DIM, (hd * qw + HEAD_DIM + half, half, -1.0), (hd * qw + HEAD_DIM, half), LANES - qw)]
    kn = [s for hd in range(N_GROUP) for s in ((hd * LANES, HEAD_DIM), pad)]
    vv = [s for hd in range(N_GROUP)
          for s in (((hd * LANES + HEAD_DIM, HEAD_DIM), pad) if hd % 2 == 0 else (pad, (hd * LANES + HEAD_DIM, HEAD_DIM)))]
    return nn, nt, uq, uqrh, kn, vv


_SPECS = _layout_specs()


def _rope_tables(t):
    inv = ROPE_THETA ** (-jnp.arange(0, MLA_ROPE, 2, dtype=F32) / MLA_ROPE)
    ang = jnp.arange(t, dtype=F32)[:, None] * inv[None, :]
    cos, sin = jnp.cos(ang), jnp.sin(ang)
    z_lo = jnp.zeros((t, HEAD_DIM), F32)
    z_hi = jnp.zeros((t, LANES - HEAD_DIM - MLA_ROPE), F32)
    cosq = jnp.concatenate([jnp.ones((t, HEAD_DIM), F32), cos, cos, z_hi], axis=1)
    sin_tab = jnp.concatenate([z_lo, sin, sin, z_hi], axis=1)
    cosk = jnp.concatenate([z_lo, cos, cos, z_hi], axis=1)
    return cosq.T, sin_tab.T, cosk, sin_tab


def kernel(x, norm_ffn1, ffn1_gate, ffn1_up, ffn1_down, norm_mix, w_in, forget_bias, mla_q_norm,
           mla_kv_norm, w_uq, w_ukv, w_out, norm_ffn2, ffn2_gate, ffn2_up, ffn2_down, norm_final):
    b, t, d = x.shape
    depth = w_in.shape[0]
    assert d == D_MODEL and t % TILE == 0 and (b * t) % FFN_TILE == 0
    nn_spec, nt_spec, uq_spec, uqrh_spec, kn_spec, vv_spec = _SPECS
    cosq, sinq, cosk, sink = _rope_tables(t)
    tri = jnp.tril(jnp.ones((TILE, TILE), F32)).astype(BF16)
    dil_bias = jnp.asarray(_dil_bias_table())
    whole = [dil for _, dil in DIL_PAIRS if t // dil // DIL_SPAN == 2]
    dil_bias2 = jnp.asarray(_dil_bias_whole(whole[0] if whole else 1))
    lane = np.arange(LANES)
    in_f = (lane >= FG_A0) & (lane < FG_END)
    fg_head = np.where(in_f, ((lane - FG_A0) // N_PIECE) % N_GROUP, 0)
    fg_piece = np.where(in_f, (lane - FG_A0) % N_PIECE, -1).astype(np.int32)
    pq, pk = jnp.asarray(fg_piece[None, :]), jnp.asarray(fg_piece[:, None])
    bf = lambda w: w.astype(BF16)
    row = lambda v: v.reshape(1, -1)

    xf = x.reshape(b * t, d)
    for l in range(depth):
        xf = _ffn_call(xf, None, None, row(norm_ffn1[l]), ffn1_gate[l], ffn1_up[l], ffn1_down[l], None)
        w_l = bf(w_in[l])
        wnn = _cols(w_l, nn_spec)
        wnt = _cols(w_l, nt_spec).T
        fb = jnp.where(jnp.asarray(in_f), jnp.take(forget_bias[l], jnp.asarray(fg_head)), 0.0)[None, :]
        wuq_l, wukv_l = bf(w_uq[l]), bf(w_ukv[l])
        wuqt = _cols(wuq_l, uq_spec).T
        wuqrht = _cols(wuq_l, uqrh_spec).T
        wkn = _cols(wukv_l, kn_spec)
        wvt = _cols(wukv_l, vv_spec).T
        (kr_a, qt_a0, vt_a, kr_b, qt_b, vt_b, kr_c, qt_c, vt_c, yd) = _proj_call(
            xf.reshape(b, t, d), row(norm_mix[l]), wnn, wnt, fb, pq, pk, tri,
            row(mla_q_norm[l]), row(mla_kv_norm[l]), wuqt, wuqrht, wkn, wvt, cosq, sinq, cosk, sink)
        qt_a = _gate_call(qt_a0, kr_a)
        o_a = _attn_call(kr_a, qt_a, vt_a, "attn_moba")
        o_b = _attn_call(kr_b, qt_b, vt_b, "attn_forget")
        o_c = _attn_call(kr_c, qt_c, vt_c, "attn_mla")
        o_d = _dil_call(yd, dil_bias, dil_bias2)
        mix = [o.reshape(b * t, HW) for o in (o_a, o_b, o_c, o_d)]
        xf = _ffn_call(xf, mix, bf(w_out[l]), row(norm_ffn2[l]), ffn2_gate[l], ffn2_up[l], ffn2_down[l],
                       row(norm_final) if l == depth - 1 else None)
    return xf.reshape(b, t, d)
```

```python
import functools

import numpy as np
import jax
import jax.numpy as jnp
from jax import lax
from jax.experimental import pallas as pl
from jax.experimental.pallas import tpu as pltpu

F32 = jnp.float32
BF16 = jnp.bfloat16

D_MODEL = 1024
HEAD_DIM = 64
N_GROUP = 4
HW = N_GROUP * HEAD_DIM
LANES = 128
MOBA_BLOCK = 256
MOBA_TOPK = 3
MLA_Q_RANK = 256
MLA_KV_RANK = 128
MLA_ROPE = 32
ROPE_THETA = 10000.0
DIL_PAIRS = ((128, 1), (512, 4), (2048, 16))
DIL_SPAN = 128
D_FF = 2816
RMS_EPS = 1e-6
NEG = -1e30
LOG2E = 1.4426950408889634

N_PIECE = 3
SEL0 = 64
ONE_Q0 = 80
PIECE_Q0 = 83
FG_A0 = 64
FG_B0 = FG_A0 + N_GROUP * N_PIECE
FG_END = FG_B0 + N_GROUP * N_PIECE

TILE = 512
ATT_TQ = 1024
LAG_LIMIT = 64.0
FFN_TILE = 512
FF_CHUNK = 256
VMEM_LIMIT = 56 * 1024 * 1024

_ALIBI = [2.0 ** (-8.0 * (i + 1) / 8) for i in range(8)]
SLOPES_MOBA = _ALIBI[1::2]
SLOPES_DIL = _ALIBI[0::2]

_WIDTHS = (HW, HW, HW, HW, HW, HW, N_GROUP, MLA_Q_RANK, MLA_KV_RANK, MLA_ROPE, HW, HW, HW)
_OFF = np.concatenate([[0], np.cumsum(_WIDTHS)]).astype(np.int64)
(AQ0, AK0, AV0, BQ0, BK0, BV0, BF0, CQ0, CKV0, CKR0, DQ0, DK0, DV0, IN_WIDTH) = [int(v) for v in _OFF]


def _dot(a, b):
    return jnp.dot(a, b, preferred_element_type=F32)


def _dot_nt(a, b):
    return lax.dot_general(a, b, (((1,), (1,)), ((), ())), preferred_element_type=F32)


def _rms(x, g):
    return x * lax.rsqrt(jnp.mean(x * x, axis=-1, keepdims=True) + RMS_EPS) * g


def _split3(x):
    p0 = x.astype(BF16)
    r = x - p0.astype(F32)
    p1 = r.astype(BF16)
    p2 = (r - p1.astype(F32)).astype(BF16)
    return p0, p1, p2


def _log_sigmoid(x):
    return jnp.minimum(x, 0.0) - jnp.log(1.0 + jnp.exp(-jnp.abs(x)))


def _ffn_kernel(*refs, has_mix, final_norm):
    it = iter(refs)
    x_ref = next(it)
    if has_mix:
        o_refs = [next(it) for _ in range(4)]
        wo_ref = next(it)
    g_ref, wg_ref, wu_ref, wd_ref = next(it), next(it), next(it), next(it)
    gf_ref = next(it) if final_norm else None
    out_ref = next(it)

    x = x_ref[...]
    if has_mix:
        for i, o_ref in enumerate(o_refs):
            x = x + _dot(o_ref[...], wo_ref[i * HW:(i + 1) * HW, :])
    h = _rms(x, g_ref[...]).astype(BF16)
    y = jnp.zeros(x.shape, F32)
    for c in range(D_FF // FF_CHUNK):
        sl = slice(c * FF_CHUNK, (c + 1) * FF_CHUNK)
        gate = _dot(h, wg_ref[:, sl].astype(BF16))
        up = _dot(h, wu_ref[:, sl].astype(BF16))
        act = (gate * jax.nn.sigmoid(gate) * up).astype(BF16)
        y = y + _dot(act, wd_ref[sl, :].astype(BF16))
    x = x + 0.5 * y
    if final_norm:
        x = _rms(x, gf_ref[...])
    out_ref[...] = x


def _const_spec(shape):
    nd = len(shape)
    return pl.BlockSpec(shape, lambda *_: (0,) * nd, pipeline_mode=pl.Buffered(1))


def _layer_spec(shape, layer):
    nd = len(shape)
    return pl.BlockSpec((None,) + tuple(shape), lambda *_: (layer,) + (0,) * nd, pipeline_mode=pl.Buffered(1))


def _ffn_call(x, mix, w_out, g, wg, wu, wd, layer, g_final):
    n = x.shape[0]
    tm = FFN_TILE
    has_mix = mix is not None
    final_norm = g_final is not None
    row = lambda i: (i, 0)
    args, specs = [x], [pl.BlockSpec((tm, D_MODEL), row)]
    if has_mix:
        for o in mix:
            args.append(o)
            specs.append(pl.BlockSpec((tm, HW), row))
        args.append(w_out)
        specs.append(_const_spec((D_MODEL, D_MODEL)))
    args += [g, wg, wu, wd]
    specs += [_const_spec((1, D_MODEL)), _layer_spec((D_MODEL, D_FF), layer), _layer_spec((D_MODEL, D_FF), layer),
              _layer_spec((D_FF, D_MODEL), layer)]
    if final_norm:
        args.append(g_final)
        specs.append(_const_spec((1, D_MODEL)))
    return pl.pallas_call(
        functools.partial(_ffn_kernel, has_mix=has_mix, final_norm=final_norm),
        out_shape=jax.ShapeDtypeStruct((n, D_MODEL), F32),
        grid=(n // tm,),
        in_specs=specs,
        out_specs=pl.BlockSpec((tm, D_MODEL), row),
        compiler_params=pltpu.CompilerParams(dimension_semantics=("arbitrary",),
                                             vmem_limit_bytes=VMEM_LIMIT),
        name="ffn_mix" if has_mix else "ffn",
    )(*args)


NN_AK, NN_BK, NN_D, NN_CQ, NN_CKV, NN_F, NN_KR, NN_KRH, NN_END = 0, 512, 1024, 1792, 2048, 2176, 2304, 2432, 2560
NT_AQ, NT_AV, NT_BQ, NT_BV, NT_END = 0, 512, 768, 1280, 1536


def _pieces_select(idx, base, val):
    p0, p1, p2 = _split3(val)
    return jnp.where(idx == base, p0.astype(F32),
                     jnp.where(idx == base + 1, p1.astype(F32), p2.astype(F32)))


def _pieces_by(which, val):
    p0, p1, p2 = _split3(val)
    return jnp.where(which == 0, p0.astype(F32), jnp.where(which == 1, p1.astype(F32), p2.astype(F32)))


def _vt_with_ones(vt, head):
    row = lax.broadcasted_iota(jnp.int32, vt.shape, 0)
    keep = (row < HEAD_DIM) if head % 2 == 0 else (row >= HEAD_DIM)
    return jnp.where(keep, vt, 1.0).astype(BF16)


def _proj_kernel(x_ref, g_ref, wnn_ref, wnt_ref, fb_ref, pq_ref, pk_ref, tri_ref,
                 qn_ref, kvn_ref, wuqt_ref, wuqrht_ref, wkn_ref, wvt_ref,
                 cosq_ref, sinq_ref, cosk_ref, sink_ref,
                 kr_a, qt_a, vt_a, kr_b, qt_b, vt_b, kr_c, qt_c, vt_c, yd_ref,
                 carry, *, tm):
    i = pl.program_id(1)
    h = _rms(x_ref[0], g_ref[...]).astype(BF16)
    lane = lax.broadcasted_iota(jnp.int32, (tm, LANES), 1)
    rowi = lax.broadcasted_iota(jnp.int32, (LANES, tm), 0)
    tok_r_i = i * tm + lax.broadcasted_iota(jnp.int32, (tm, LANES), 0)
    tok_r = tok_r_i.astype(F32)
    tok_c = (i * tm + lax.broadcasted_iota(jnp.int32, (LANES, tm), 1)).astype(F32)
    c_hd = HEAD_DIM ** -0.5 * LOG2E
    head_sl = lambda hd: slice(hd * LANES, (hd + 1) * LANES)

    @pl.when(i == 0)
    def _():
        carry[...] = jnp.zeros_like(carry)

    piece_k = (lane >= ONE_Q0) & (lane < ONE_Q0 + N_PIECE)
    ones_k = (lane >= PIECE_Q0) & (lane < PIECE_Q0 + N_PIECE)
    ones_q = (rowi >= ONE_Q0) & (rowi < ONE_Q0 + N_PIECE)
    piece_q = (rowi >= PIECE_Q0) & (rowi < PIECE_Q0 + N_PIECE)
    ind = (lane >= SEL0) & (lane < ONE_Q0) & ((tok_r_i // MOBA_BLOCK) == lane - SEL0)
    zk = _dot(h, wnn_ref[:, NN_AK:NN_BK])
    zq = _dot_nt(wnt_ref[NT_AQ:NT_AV, :], h)
    zv = _dot_nt(wnt_ref[NT_AV:NT_BQ, :], h)
    for hd in range(N_GROUP):
        slope2 = SLOPES_MOBA[hd] * LOG2E
        ext = _pieces_select(lane, ONE_Q0, tok_r * slope2)
        kr_a[0, hd] = jnp.where(lane < HEAD_DIM, zk[:, head_sl(hd)],
                                jnp.where(piece_k, ext, jnp.where(ones_k | ind, 1.0, 0.0))).astype(BF16)
        ext = _pieces_select(rowi, PIECE_Q0, tok_c * (-slope2))
        qt_a[0, hd, 0] = jnp.where(rowi < HEAD_DIM, zq[head_sl(hd), :] * c_hd,
                                   jnp.where(ones_q, 1.0, jnp.where(piece_q, ext, 0.0))).astype(BF16)
        vt_a[0, hd, 0] = _vt_with_ones(zv[head_sl(hd // 2), :], hd)

    zcf = _dot(h, wnn_ref[:, NN_CKV:NN_KR])
    in_f = (lane >= FG_A0) & (lane < FG_END)
    lf = jnp.where(in_f, _log_sigmoid(zcf[:, LANES:] + fb_ref[...]), 0.0)
    p0, p1, p2 = _split3(lf)
    tri = tri_ref[...]
    dec = _dot(tri, p0) + _dot(tri, p1) + _dot(tri, p2) + carry[0:1, :]
    carry[0:1, :] = dec[tm - 1:tm, :]
    e_k = _pieces_by(pq_ref[...], dec * (-LOG2E))
    e_q = _pieces_by(pk_ref[...], dec.T * LOG2E)
    all_k = (lane >= FG_A0) & (lane < FG_B0)
    all_q = (rowi >= FG_B0) & (rowi < FG_END)
    zk = _dot(h, wnn_ref[:, NN_BK:NN_D])
    zq = _dot_nt(wnt_ref[NT_BQ:NT_BV, :], h)
    zv = _dot_nt(wnt_ref[NT_BV:NT_END, :], h)
    for hd in range(N_GROUP):
        own_k = (lane >= FG_B0 + N_PIECE * hd) & (lane < FG_B0 + N_PIECE * (hd + 1))
        own_q = (rowi >= FG_A0 + N_PIECE * hd) & (rowi < FG_A0 + N_PIECE * (hd + 1))
        kr_b[0, hd] = jnp.where(lane < HEAD_DIM, zk[:, head_sl(hd)],
                                jnp.where(all_k, e_k, jnp.where(own_k, 1.0, 0.0))).astype(BF16)
        qt_b[0, hd, 0] = jnp.where(rowi < HEAD_DIM, zq[head_sl(hd), :] * c_hd,
                                   jnp.where(all_q, e_q, jnp.where(own_q, 1.0, 0.0))).astype(BF16)
        vt_b[0, hd, 0] = _vt_with_ones(zv[head_sl(hd // 2), :], hd)

    c_mla = (HEAD_DIM + MLA_ROPE) ** -0.5 * LOG2E
    cqn = _rms(_dot(h, wnn_ref[:, NN_CQ:NN_CKV]), qn_ref[...]).astype(BF16)
    ckvn = _rms(zcf[:, :LANES], kvn_ref[...]).astype(BF16)
    zr = _dot(h, wnn_ref[:, NN_KR:NN_END])
    rot_k = zr[:, :LANES] * cosk_ref[...] + zr[:, LANES:] * sink_ref[...]
    kn = _dot(ckvn, wkn_ref[...])
    qt = _dot_nt(wuqt_ref[...], cqn)
    qrht = _dot_nt(wuqrht_ref[...], cqn)
    vt = _dot_nt(wvt_ref[...], ckvn)
    cosq, sinq = cosq_ref[...], sinq_ref[...]
    for hd in range(N_GROUP):
        kr_c[0, hd] = (kn[:, head_sl(hd)] + rot_k).astype(BF16)
        qt_c[0, hd, 0] = ((qt[head_sl(hd), :] * cosq + qrht[head_sl(hd), :] * sinq) * c_mla).astype(BF16)
        vt_c[0, hd, 0] = _vt_with_ones(vt[head_sl(hd // 2), :], hd)

    yd_ref[0] = _dot(h, wnn_ref[:, NN_D:NN_CQ])


def _proj_call(x3, g, wnn, wnt, fb, pq, pk, tri, qn, kvn, wuqt, wuqrht, wkn, wvt, cosq, sinq, cosk, sink):
    b, t, _ = x3.shape
    tm = TILE
    nk = t // tm
    per_q = ATT_TQ // tm
    kr_shape = jax.ShapeDtypeStruct((b, N_GROUP, t, LANES), BF16)
    vt_shape = jax.ShapeDtypeStruct((b, N_GROUP, nk, LANES, tm), BF16)
    qt_shape = jax.ShapeDtypeStruct((b, N_GROUP, t // ATT_TQ, LANES, ATT_TQ), BF16)
    kr_spec = pl.BlockSpec((1, N_GROUP, tm, LANES), lambda bi, i: (bi, 0, i, 0))
    vt_spec = pl.BlockSpec((1, N_GROUP, 1, LANES, tm), lambda bi, i: (bi, 0, i, 0, 0))
    qt_spec = pl.BlockSpec((1, N_GROUP, 1, LANES, tm), lambda bi, i: (bi, 0, i // per_q, 0, i % per_q))
    in_specs = [
        pl.BlockSpec((1, tm, D_MODEL), lambda bi, i: (bi, i, 0)),
        _const_spec((1, D_MODEL)),
        _const_spec((D_MODEL, NN_END)),
        _const_spec((NT_END, D_MODEL)),
        _const_spec((1, LANES)),
        _const_spec((1, LANES)),
        _const_spec((LANES, 1)),
        _const_spec((tm, tm)),
        _const_spec((1, MLA_Q_RANK)),
        _const_spec((1, MLA_KV_RANK)),
        _const_spec((N_GROUP * LANES, MLA_Q_RANK)),
        _const_spec((N_GROUP * LANES, MLA_Q_RANK)),
        _const_spec((MLA_KV_RANK, N_GROUP * LANES)),
        _const_spec((HW, MLA_KV_RANK)),
        pl.BlockSpec((LANES, tm), lambda bi, i: (0, i)),
        pl.BlockSpec((LANES, tm), lambda bi, i: (0, i)),
        pl.BlockSpec((tm, LANES), lambda bi, i: (i, 0)),
        pl.BlockSpec((tm, LANES), lambda bi, i: (i, 0)),
    ]
    out_shape = [kr_shape, qt_shape, vt_shape] * 3 + [jax.ShapeDtypeStruct((b, t, 3 * HW), F32)]
    out_specs = [kr_spec, qt_spec, vt_spec] * 3 + [pl.BlockSpec((1, tm, 3 * HW), lambda bi, i: (bi, i, 0))]
    return pl.pallas_call(
        functools.partial(_proj_kernel, tm=tm),
        out_shape=out_shape,
        grid=(b, nk),
        in_specs=in_specs,
        out_specs=out_specs,
        scratch_shapes=[pltpu.VMEM((8, LANES), F32)],
        compiler_params=pltpu.CompilerParams(dimension_semantics=("arbitrary", "arbitrary"),
                                             vmem_limit_bytes=VMEM_LIMIT),
        name="proj",
    )(x3, g, wnn, wnt, fb, pq, pk, tri, qn, kvn, wuqt, wuqrht, wkn, wvt, cosq, sinq, cosk, sink)


def _gate_kernel(qt0_ref, kr_ref, qt_ref, *, t, tq):
    nq = t // tq
    nblk = t // MOBA_BLOCK
    blk = lax.broadcasted_iota(jnp.int32, (nblk, tq), 0)
    lane = lax.broadcasted_iota(jnp.int32, (nblk, LANES), 1)
    for hd in range(N_GROUP):
        ksum = jnp.zeros((nblk, LANES), F32)
        for j in range(nq):
            key_blk = (j * tq + lax.broadcasted_iota(jnp.int32, (nblk, tq), 1)) // MOBA_BLOCK
            ind = jnp.where(key_blk == blk, 1.0, 0.0).astype(BF16)
            ksum = ksum + _dot(ind, kr_ref[0, hd, j * tq:(j + 1) * tq, :])
        kmean = jnp.where(lane < HEAD_DIM, ksum * (1.0 / MOBA_BLOCK), 0.0).astype(BF16)
        for qi in range(nq):
            q0 = qt0_ref[0, hd, qi]
            cur = (qi * tq + lax.broadcasted_iota(jnp.int32, (nblk, tq), 1)) // MOBA_BLOCK
            valid = blk < cur
            gate = jnp.where(valid, _dot(kmean, q0), -jnp.inf)
            rank = jnp.zeros((nblk, tq), F32)
            for m in range(nblk):
                gm = gate[m:m + 1, :]
                ahead = (gm > gate) | ((gm == gate) & (blk > m))
                rank = rank + jnp.where(ahead, 1.0, 0.0)
            keep = (valid & (rank < MOBA_TOPK)) | (blk == cur)
            selb = jnp.where(keep, 0.0, NEG).astype(BF16)
            qt_ref[0, hd, qi] = jnp.concatenate([q0[:SEL0], selb, q0[SEL0 + nblk:]], axis=0)


def _gate_call(qt0, kr):
    b, _, nq, _, tq = qt0.shape
    t = nq * tq
    nblk = t // MOBA_BLOCK
    assert nblk % 16 == 0 and nblk <= ONE_Q0 - SEL0
    qt_spec = pl.BlockSpec((1, N_GROUP, nq, LANES, tq), lambda bi: (bi, 0, 0, 0, 0))
    kr_spec = pl.BlockSpec((1, N_GROUP, t, LANES), lambda bi: (bi, 0, 0, 0))
    return pl.pallas_call(
        functools.partial(_gate_kernel, t=t, tq=tq),
        out_shape=jax.ShapeDtypeStruct(qt0.shape, BF16),
        grid=(b,),
        in_specs=[qt_spec, kr_spec],
        out_specs=qt_spec,
        compiler_params=pltpu.CompilerParams(dimension_semantics=("arbitrary",),
                                             vmem_limit_bytes=VMEM_LIMIT),
        name="moba_gate",
    )(qt0, kr)


def _attn_kernel(kr_ref, qt_ref, vt_ref, o_ref, acc_ref, *, tq, tk):
    qi = pl.program_id(1)
    ratio = tq // tk
    rowi = lax.broadcasted_iota(jnp.int32, (LANES, tq), 0)
    acc_ref[...] = jnp.zeros(acc_ref.shape, F32)

    def step(j, ms, c0, masked):
        w = tq - c0
        k0 = pl.multiple_of(j * tk, tk)
        sts = [_dot(kr_ref[0, hd, pl.ds(k0, tk), :], qt_ref[0, hd, 0, :, c0:]) for hd in range(N_GROUP)]
        out = []
        for hd in range(N_GROUP):
            st = sts[hd]
            if masked:
                keyi = lax.broadcasted_iota(jnp.int32, (tk, w), 0)
                qryi = lax.broadcasted_iota(jnp.int32, (tk, w), 1)
                st = jnp.where(keyi <= qryi, st, NEG)
            m_old = ms[hd][:, c0:]
            m_new = jnp.maximum(m_old, jnp.max(st, axis=0, keepdims=True))
            p = jnp.exp2(st - m_new)
            pv = _dot(vt_ref[0, hd, j], p.astype(BF16))
            acc_ref[hd, :, c0:] = jnp.exp2(m_old - m_new) * acc_ref[hd, :, c0:] + pv
            out.append(m_new if c0 == 0 else jnp.concatenate([ms[hd][:, :c0], m_new], axis=1))
        return tuple(out)

    def lagged_step(j, carry):
        used, nxt, excess = carry
        k0 = pl.multiple_of(j * tk, tk)
        sts = [_dot(kr_ref[0, hd, pl.ds(k0, tk), :], qt_ref[0, hd, 0]) for hd in range(N_GROUP)]
        new_nxt = []
        for hd in range(N_GROUP):
            st, ref = sts[hd], nxt[hd]
            p = jnp.exp2(st - ref)
            block_max = jnp.max(st, axis=0, keepdims=True)
            pv = _dot(vt_ref[0, hd, j], p.astype(BF16))
            acc_ref[hd] = jnp.exp2(used[hd] - ref) * acc_ref[hd] + pv
            excess = jnp.maximum(excess, block_max - ref)
            new_nxt.append(jnp.maximum(ref, block_max))
        return nxt, tuple(new_nxt), excess

    def diagonal(ms):
        for d in range(ratio):
            ms = step(qi * ratio + d, ms, d * tk, True)
        return ms

    m0 = tuple(jnp.full((1, tq), -jnp.inf, F32) for _ in range(N_GROUP))
    ms = diagonal(m0)
    _, _, excess = lax.fori_loop(0, qi * ratio, lagged_step, (ms, ms, jnp.full((1, tq), -jnp.inf, F32)))

    @pl.when(jnp.max(excess) > LAG_LIMIT)
    def _():
        acc_ref[...] = jnp.zeros(acc_ref.shape, F32)
        diagonal(lax.fori_loop(0, qi * ratio, lambda j, c: step(j, c, 0, False), m0))

    for pair in range(N_GROUP // 2):
        even, odd = acc_ref[2 * pair], acc_ref[2 * pair + 1]
        out_t = jnp.where(rowi < HEAD_DIM, even / even[HEAD_DIM:HEAD_DIM + 1, :], odd / odd[0:1, :])
        o_ref[0, :, pair * LANES:(pair + 1) * LANES] = out_t.T.astype(BF16)


def _attn_call(kr, qt, vt, name):
    b, _, t, _ = kr.shape
    tq, tk = qt.shape[-1], vt.shape[-1]
    return pl.pallas_call(
        functools.partial(_attn_kernel, tq=tq, tk=tk),
        out_shape=jax.ShapeDtypeStruct((b, t, HW), BF16),
        grid=(b, t // tq),
        in_specs=[
            pl.BlockSpec((1, N_GROUP, t, LANES), lambda bi, i: (bi, 0, 0, 0)),
            pl.BlockSpec((1, N_GROUP, 1, LANES, tq), lambda bi, i: (bi, 0, i, 0, 0)),
            pl.BlockSpec((1, N_GROUP, t // tk, LANES, tk), lambda bi, i: (bi, 0, 0, 0, 0)),
        ],
        out_specs=pl.BlockSpec((1, tq, HW), lambda bi, i: (bi, i, 0)),
        scratch_shapes=[pltpu.VMEM((N_GROUP, LANES, tq), F32)],
        compiler_params=pltpu.CompilerParams(dimension_semantics=("arbitrary", "arbitrary"),
                                             vmem_limit_bytes=VMEM_LIMIT),
        name=name,
    )(kr, qt, vt)


DIL_UNROLL = 8


def _dil_kernel(q_ref, k_ref, v_ref, bias_ref, bias2_ref, o_ref, acc_ref, max_ref, *, t):
    span = DIL_SPAN
    c_hd = HEAD_DIM ** -0.5 * LOG2E

    def rows(start, size, dil):
        return pl.ds(start, size) if dil == 1 else pl.ds(start, size, stride=dil)

    def own(shape, hh):
        lane = lax.broadcasted_iota(jnp.int32, shape, 1)
        return (lane < HEAD_DIM) if hh == 0 else (lane >= HEAD_DIM)

    def finish_blocks(work, br, dil):
        for hh, q0, nq, v2, bias, qk in work:
            va = jnp.where(own(v2.shape, hh), v2, 1.0).astype(BF16)
            s = qk + bias
            m = jnp.max(s, axis=1, keepdims=True)
            p = jnp.exp2(s - m)
            acc_ref[br, hh, rows(q0, nq, dil), :] = _dot(p.astype(BF16), va)
            max_ref[br, hh, rows(q0, nq, dil), :] = jnp.broadcast_to(m, (nq, LANES))

    for br, (window, dil) in enumerate(DIL_PAIRS):
        nb = t // dil // span

        def group(it, carry, br=br, dil=dil, nb=nb):
            work = []
            for u in range(DIL_UNROLL):
                idx = it * DIL_UNROLL + u
                r, n = idx // nb, idx % nb
                q0 = r + n * (span * dil)
                first = 1 - jnp.minimum(n, 1)
                k0 = q0 - (1 - first) * (span * dil)
                q2 = q_ref[0, rows(q0, span, dil), :] * c_hd
                k2 = k_ref[0, rows(k0, 2 * span, dil), :].astype(BF16)
                v2 = v_ref[0, rows(k0, 2 * span, dil), :]
                for hh in range(2):
                    qz = jnp.where(own(q2.shape, hh), q2, 0.0).astype(BF16)
                    work.append((hh, q0, span, v2, bias_ref[hh, 2 * br + first], _dot_nt(qz, k2)))
            finish_blocks(work, br, dil)
            return carry

        def group_whole(it, carry, br=br, dil=dil):
            work = []
            for u in range(DIL_UNROLL // 2):
                r = it * (DIL_UNROLL // 2) + u
                q2 = q_ref[0, rows(r, 2 * span, dil), :] * c_hd
                k2 = k_ref[0, rows(r, 2 * span, dil), :].astype(BF16)
                v2 = v_ref[0, rows(r, 2 * span, dil), :]
                for hh in range(2):
                    qz = jnp.where(own(q2.shape, hh), q2, 0.0).astype(BF16)
                    work.append((hh, r, 2 * span, v2, bias2_ref[hh], _dot_nt(qz, k2)))
            finish_blocks(work, br, dil)
            return carry

        if nb == 2:
            lax.fori_loop(0, dil // (DIL_UNROLL // 2), group_whole, 0)
        else:
            lax.fori_loop(0, dil * nb // DIL_UNROLL, group, 0)

    chunk = 256
    lane_c = lax.broadcasted_iota(jnp.int32, (chunk, LANES), 1)

    def finish(ci, carry):
        sl = pl.ds(ci * chunk, chunk)
        outs = []
        for hh in range(2):
            ms = [max_ref[br, hh, sl, :] for br in range(len(DIL_PAIRS))]
            mx = functools.reduce(jnp.maximum, ms)
            tot = sum(acc_ref[br, hh, sl, :] * jnp.exp2(m - mx) for br, m in enumerate(ms))
            outs.append(tot / pltpu.roll(tot, HEAD_DIM, axis=1))
        o_ref[0, sl, :] = jnp.where(lane_c < HEAD_DIM, outs[0], outs[1]).astype(BF16)
        return carry

    lax.fori_loop(0, t // chunk, finish, 0)


def _dil_bias_table():
    qi = np.arange(DIL_SPAN)[:, None]
    kj = np.arange(2 * DIL_SPAN)[None, :]
    tab = np.zeros((N_GROUP, 2 * len(DIL_PAIRS), DIL_SPAN, 2 * DIL_SPAN), np.float32)
    for variant in range(2):
        delta = qi - kj + (DIL_SPAN if variant == 0 else 0)
        valid = (delta >= 0) & (delta <= DIL_SPAN)
        for hd in range(N_GROUP):
            for br, (_, dil) in enumerate(DIL_PAIRS):
                tab[hd, 2 * br + variant] = np.where(valid, -SLOPES_DIL[hd] * LOG2E * (delta * dil), NEG)
    return tab


def _dil_bias_whole(dil):
    delta = np.arange(2 * DIL_SPAN)[:, None] - np.arange(2 * DIL_SPAN)[None, :]
    valid = (delta >= 0) & (delta <= DIL_SPAN)
    return np.stack([np.where(valid, -SLOPES_DIL[hd] * LOG2E * (delta * dil), NEG)
                     for hd in range(N_GROUP)]).astype(np.float32)


def _dil_call(yd, bias, bias2):
    b, t, _ = yd.shape
    for window, dil in DIL_PAIRS:
        nb = t // dil // DIL_SPAN
        assert window // dil == DIL_SPAN and nb * dil * DIL_SPAN == t and nb >= 2
        assert (dil * nb) % DIL_UNROLL == 0
    n_pair = N_GROUP // 2
    n_br = len(DIL_PAIRS)
    return pl.pallas_call(
        functools.partial(_dil_kernel, t=t),
        out_shape=jax.ShapeDtypeStruct((b, t, HW), BF16),
        grid=(b, n_pair),
        in_specs=[
            pl.BlockSpec((1, t, LANES), lambda bi, p: (bi, 0, p)),
            pl.BlockSpec((1, t, LANES), lambda bi, p: (bi, 0, n_pair + p)),
            pl.BlockSpec((1, t, LANES), lambda bi, p: (bi, 0, 2 * n_pair + p)),
            pl.BlockSpec((2, 2 * n_br, DIL_SPAN, 2 * DIL_SPAN), lambda bi, p: (p, 0, 0, 0)),
            pl.BlockSpec((2, 2 * DIL_SPAN, 2 * DIL_SPAN), lambda bi, p: (p, 0, 0)),
        ],
        out_specs=pl.BlockSpec((1, t, LANES), lambda bi, p: (bi, 0, p)),
        scratch_shapes=[pltpu.VMEM((n_br, 2, t, LANES), F32), pltpu.VMEM((n_br, 2, t, LANES), F32)],
        compiler_params=pltpu.CompilerParams(dimension_semantics=("arbitrary", "arbitrary"),
                                             vmem_limit_bytes=VMEM_LIMIT),
        name="dilated",
    )(yd, yd, yd, bias, bias2)


def _cols(w, spec):
    parts = []
    for item in spec:
        if isinstance(item, int):
            parts.append(jnp.zeros((w.shape[0], item), w.dtype))
        else:
            piece = w[:, item[0]:item[0] + item[1]]
            parts.append(-piece if len(item) == 3 else piece)
    return jnp.concatenate(parts, axis=1)


def _layout_specs():
    half = MLA_ROPE // 2
    pad = LANES - HEAD_DIM
    k_heads = lambda base: [s for hd in range(N_GROUP) for s in ((base + hd * HEAD_DIM, HEAD_DIM), pad)]
    v_heads = lambda base: [(base, HW)]
    forget = [(BF0 + hd, 1) for hd in range(N_GROUP) for _ in range(N_PIECE)]
    rope_pad = LANES - HEAD_DIM - MLA_ROPE
    nn = (k_heads(AK0) + k_heads(BK0) + [(DQ0, 3 * HW), (CQ0, MLA_Q_RANK), (CKV0, MLA_KV_RANK)]
          + [FG_A0] + forget + forget + [LANES - FG_END]
          + [HEAD_DIM, (CKR0, MLA_ROPE), rope_pad]
          + [HEAD_DIM, (CKR0 + half, half, -1.0), (CKR0, half), rope_pad])
    nt = k_heads(AQ0) + v_heads(AV0) + k_heads(BQ0) + v_heads(BV0)
    qw = HEAD_DIM + MLA_ROPE
    uq = [s for hd in range(N_GROUP) for s in ((hd * qw, qw), LANES - qw)]
    uqrh = [s for hd in range(N_GROUP)
            for s in (HEAD_DIM, (hd * qw + HEAD_DIM + half, half, -1.0), (hd * qw + HEAD_DIM, half), LANES - qw)]
    kn = [s for hd in range(N_GROUP) for s in ((hd * LANES, HEAD_DIM), pad)]
    vv = [(hd * LANES + HEAD_DIM, HEAD_DIM) for hd in range(N_GROUP)]
    return nn, nt, uq, uqrh, kn, vv


_SPECS = _layout_specs()


def _rope_tables(t):
    inv = ROPE_THETA ** (-jnp.arange(0, MLA_ROPE, 2, dtype=F32) / MLA_ROPE)
    ang = jnp.arange(t, dtype=F32)[:, None] * inv[None, :]
    cos, sin = jnp.cos(ang), jnp.sin(ang)
    z_lo = jnp.zeros((t, HEAD_DIM), F32)
    z_hi = jnp.zeros((t, LANES - HEAD_DIM - MLA_ROPE), F32)
    cosq = jnp.concatenate([jnp.ones((t, HEAD_DIM), F32), cos, cos, z_hi], axis=1)
    sin_tab = jnp.concatenate([z_lo, sin, sin, z_hi], axis=1)
    cosk = jnp.concatenate([z_lo, cos, cos, z_hi], axis=1)
    return cosq.T, sin_tab.T, cosk, sin_tab


def kernel(x, norm_ffn1, ffn1_gate, ffn1_up, ffn1_down, norm_mix, w_in, forget_bias, mla_q_norm,
           mla_kv_norm, w_uq, w_ukv, w_out, norm_ffn2, ffn2_gate, ffn2_up, ffn2_down, norm_final):
    b, t, d = x.shape
    depth = w_in.shape[0]
    assert d == D_MODEL and t % ATT_TQ == 0 and ATT_TQ % TILE == 0 and (b * t) % FFN_TILE == 0
    nn_spec, nt_spec, uq_spec, uqrh_spec, kn_spec, vv_spec = _SPECS
    cosq, sinq, cosk, sink = _rope_tables(t)
    tri = jnp.tril(jnp.ones((TILE, TILE), F32)).astype(BF16)
    dil_bias = jnp.asarray(_dil_bias_table())
    whole = [dil for _, dil in DIL_PAIRS if t // dil // DIL_SPAN == 2]
    dil_bias2 = jnp.asarray(_dil_bias_whole(whole[0] if whole else 1))
    lane = np.arange(LANES)
    in_f = (lane >= FG_A0) & (lane < FG_END)
    fg_head = np.where(in_f, ((lane - FG_A0) // N_PIECE) % N_GROUP, 0)
    fg_piece = np.where(in_f, (lane - FG_A0) % N_PIECE, -1).astype(np.int32)
    pq, pk = jnp.asarray(fg_piece[None, :]), jnp.asarray(fg_piece[:, None])
    bf = lambda w: w.astype(BF16)
    row = lambda v: v.reshape(1, -1)

    xf = x.reshape(b * t, d)
    for l in range(depth):
        xf = _ffn_call(xf, None, None, row(norm_ffn1[l]), ffn1_gate, ffn1_up, ffn1_down, l, None)
        w_l = bf(w_in[l])
        wnn = _cols(w_l, nn_spec)
        wnt = _cols(w_l, nt_spec).T
        fb = jnp.where(jnp.asarray(in_f), jnp.take(forget_bias[l], jnp.asarray(fg_head)), 0.0)[None, :]
        wuq_l, wukv_l = bf(w_uq[l]), bf(w_ukv[l])
        wuqt = _cols(wuq_l, uq_spec).T
        wuqrht = _cols(wuq_l, uqrh_spec).T
        wkn = _cols(wukv_l, kn_spec)
        wvt = _cols(wukv_l, vv_spec).T
        (kr_a, qt_a0, vt_a, kr_b, qt_b, vt_b, kr_c, qt_c, vt_c, yd) = _proj_call(
            xf.reshape(b, t, d), row(norm_mix[l]), wnn, wnt, fb, pq, pk, tri,
            row(mla_q_norm[l]), row(mla_kv_norm[l]), wuqt, wuqrht, wkn, wvt, cosq, sinq, cosk, sink)
        qt_a = _gate_call(qt_a0, kr_a)
        o_a = _attn_call(kr_a, qt_a, vt_a, "attn_moba")
        o_b = _attn_call(kr_b, qt_b, vt_b, "attn_forget")
        o_c = _attn_call(kr_c, qt_c, vt_c, "attn_mla")
        o_d = _dil_call(yd, dil_bias, dil_bias2)
        mix = [o.reshape(b * t, HW) for o in (o_a, o_b, o_c, o_d)]
        xf = _ffn_call(xf, mix, bf(w_out[l]), row(norm_ffn2[l]), ffn2_gate, ffn2_up, ffn2_down, l,
                       row(norm_final) if l == depth - 1 else None)
    return xf.reshape(b, t, d)
```

```python
import functools

import numpy as np
import jax
import jax.numpy as jnp
from jax import lax
from jax.experimental import pallas as pl
from jax.experimental.pallas import tpu as pltpu

F32 = jnp.float32
BF16 = jnp.bfloat16

D_MODEL = 1024
HEAD_DIM = 64
N_GROUP = 4
HW = N_GROUP * HEAD_DIM
LANES = 128
MOBA_BLOCK = 256
MOBA_TOPK = 3
MLA_Q_RANK = 256
MLA_KV_RANK = 128
MLA_ROPE = 32
ROPE_THETA = 10000.0
DIL_PAIRS = ((128, 1), (512, 4), (2048, 16))
DIL_SPAN = 128
D_FF = 2816
RMS_EPS = 1e-6
NEG = -1e30
LOG2E = 1.4426950408889634

N_PIECE = 3
MOBA_NBLK = 16
SEL_OFF = 0
ONE_OFF = 16
PIECE_OFF = 19
FGA_OFF = 0
FGB_OFF = N_GROUP * N_PIECE
FG_W = 2 * N_GROUP * N_PIECE

TILE = 512
ATT_TQ = 1024
LAG_LIMIT = 64.0
FFN_TILE = 512
FF_CHUNK = 256
VMEM_LIMIT = 56 * 1024 * 1024

_ALIBI = [2.0 ** (-8.0 * (i + 1) / 8) for i in range(8)]
SLOPES_MOBA = _ALIBI[1::2]
SLOPES_DIL = _ALIBI[0::2]

_WIDTHS = (HW, HW, HW, HW, HW, HW, N_GROUP, MLA_Q_RANK, MLA_KV_RANK, MLA_ROPE, HW, HW, HW)
_OFF = np.concatenate([[0], np.cumsum(_WIDTHS)]).astype(np.int64)
(AQ0, AK0, AV0, BQ0, BK0, BV0, BF0, CQ0, CKV0, CKR0, DQ0, DK0, DV0, IN_WIDTH) = [int(v) for v in _OFF]


def _dot(a, b):
    return jnp.dot(a, b, preferred_element_type=F32)


def _dot_nt(a, b):
    return lax.dot_general(a, b, (((1,), (1,)), ((), ())), preferred_element_type=F32)


def _rms(x, g):
    return x * lax.rsqrt(jnp.mean(x * x, axis=-1, keepdims=True) + RMS_EPS) * g


def _split3(x):
    p0 = x.astype(BF16)
    r = x - p0.astype(F32)
    p1 = r.astype(BF16)
    p2 = (r - p1.astype(F32)).astype(BF16)
    return p0, p1, p2


def _log_sigmoid(x):
    return jnp.minimum(x, 0.0) - jnp.log(1.0 + jnp.exp(-jnp.abs(x)))


def _ffn_kernel(*refs, has_mix, final_norm):
    it = iter(refs)
    x_ref = next(it)
    if has_mix:
        o_refs = [next(it) for _ in range(4)]
        wo_ref = next(it)
    g_ref, wg_ref, wu_ref, wd_ref = next(it), next(it), next(it), next(it)
    gf_ref = next(it) if final_norm else None
    out_ref = next(it)

    x = x_ref[...]
    if has_mix:
        for i, o_ref in enumerate(o_refs):
            x = x + _dot(o_ref[...], wo_ref[i * HW:(i + 1) * HW, :])
    h = _rms(x, g_ref[...]).astype(BF16)
    y = jnp.zeros(x.shape, F32)
    for c in range(D_FF // FF_CHUNK):
        sl = slice(c * FF_CHUNK, (c + 1) * FF_CHUNK)
        gate = _dot(h, wg_ref[:, sl].astype(BF16))
        up = _dot(h, wu_ref[:, sl].astype(BF16))
        act = (gate * jax.nn.sigmoid(gate) * up).astype(BF16)
        y = y + _dot(act, wd_ref[sl, :].astype(BF16))
    x = x + 0.5 * y
    if final_norm:
        x = _rms(x, gf_ref[...])
    out_ref[...] = x


def _const_spec(shape):
    nd = len(shape)
    return pl.BlockSpec(shape, lambda *_: (0,) * nd, pipeline_mode=pl.Buffered(1))


def _layer_spec(shape, layer):
    nd = len(shape)
    return pl.BlockSpec((None,) + tuple(shape), lambda *_: (layer,) + (0,) * nd, pipeline_mode=pl.Buffered(1))


def _ffn_call(x, mix, w_out, g, wg, wu, wd, layer, g_final):
    n = x.shape[0]
    tm = FFN_TILE
    has_mix = mix is not None
    final_norm = g_final is not None
    row = lambda i: (i, 0)
    args, specs = [x], [pl.BlockSpec((tm, D_MODEL), row)]
    if has_mix:
        for o in mix:
            args.append(o)
            specs.append(pl.BlockSpec((tm, HW), row))
        args.append(w_out)
        specs.append(_const_spec((D_MODEL, D_MODEL)))
    args += [g, wg, wu, wd]
    specs += [_const_spec((1, D_MODEL)), _layer_spec((D_MODEL, D_FF), layer), _layer_spec((D_MODEL, D_FF), layer),
              _layer_spec((D_FF, D_MODEL), layer)]
    if final_norm:
        args.append(g_final)
        specs.append(_const_spec((1, D_MODEL)))
    return pl.pallas_call(
        functools.partial(_ffn_kernel, has_mix=has_mix, final_norm=final_norm),
        out_shape=jax.ShapeDtypeStruct((n, D_MODEL), F32),
        grid=(n // tm,),
        in_specs=specs,
        out_specs=pl.BlockSpec((tm, D_MODEL), row),
        compiler_params=pltpu.CompilerParams(dimension_semantics=("arbitrary",),
                                             vmem_limit_bytes=VMEM_LIMIT),
        name="ffn_mix" if has_mix else "ffn",
    )(*args)


NN_AK, NN_BK, NN_D, NN_CQ, NN_CKV, NN_F, NN_KR, NN_KRH, NN_END = 0, 256, 512, 1280, 1536, 1664, 1792, 1920, 2048
NT_AQ, NT_AV, NT_BQ, NT_BV, NT_END = 0, 256, 512, 768, 1024


def _pieces_select(idx, base, val):
    p0, p1, p2 = _split3(val)
    return jnp.where(idx == base, p0.astype(F32),
                     jnp.where(idx == base + 1, p1.astype(F32), p2.astype(F32)))


def _pieces_by(which, val):
    p0, p1, p2 = _split3(val)
    return jnp.where(which == 0, p0.astype(F32), jnp.where(which == 1, p1.astype(F32), p2.astype(F32)))


def _vt_with_ones(vt, head):
    row = lax.broadcasted_iota(jnp.int32, vt.shape, 0)
    keep = (row < HEAD_DIM) if head % 2 == 0 else (row >= HEAD_DIM)
    return jnp.where(keep, vt, 1.0).astype(BF16)


def _proj_kernel(x_ref, g_ref, wnn_ref, wnt_ref, fb_ref, pq_ref, pk_ref, tri_ref,
                 qn_ref, kvn_ref, wuqt_ref, wuqrht_ref, wkn_ref, wvt_ref,
                 cosq_ref, sinq_ref, cosk_ref, sink_ref,
                 kr_a, qt_a, vt_a, kr_b, qt_b, vt_b, kr_c, qt_c, vt_c, yd_ref,
                 carry, *, tm):
    i = pl.program_id(1)
    h = _rms(x_ref[0], g_ref[...]).astype(BF16)
    lane = lax.broadcasted_iota(jnp.int32, (tm, LANES), 1)
    rowi = lax.broadcasted_iota(jnp.int32, (LANES, tm), 0)
    tok_r_i = i * tm + lax.broadcasted_iota(jnp.int32, (tm, LANES), 0)
    tok_r = tok_r_i.astype(F32)
    tok_c = (i * tm + lax.broadcasted_iota(jnp.int32, (LANES, tm), 1)).astype(F32)
    c_hd = HEAD_DIM ** -0.5 * LOG2E
    head_sl = lambda hd: slice(hd * LANES, (hd + 1) * LANES)

    @pl.when(i == 0)
    def _():
        carry[...] = jnp.zeros_like(carry)

    def feature_masks(idx, hd):
        qk0 = HEAD_DIM * (hd % 2)
        return (idx >= qk0) & (idx < qk0 + HEAD_DIM), idx - (HEAD_DIM - qk0)

    between = lambda off, lo, n: (off >= lo) & (off < lo + n)

    zk = _dot(h, wnn_ref[:, NN_AK:NN_BK])
    zq = _dot_nt(wnt_ref[NT_AQ:NT_AV, :], h)
    zv = _dot_nt(wnt_ref[NT_AV:NT_BQ, :], h)
    for hd in range(N_GROUP):
        slope2 = SLOPES_MOBA[hd] * LOG2E
        pair = head_sl(hd // 2)
        is_k, off = feature_masks(lane, hd)
        ind = between(off, SEL_OFF, MOBA_NBLK) & ((tok_r_i // MOBA_BLOCK) == off - SEL_OFF)
        ext = _pieces_select(off, ONE_OFF, tok_r * slope2)
        kr_a[0, hd] = jnp.where(is_k, zk[:, pair],
                                jnp.where(between(off, ONE_OFF, N_PIECE), ext,
                                          jnp.where(between(off, PIECE_OFF, N_PIECE) | ind, 1.0, 0.0))).astype(BF16)
        is_q, off = feature_masks(rowi, hd)
        ext = _pieces_select(off, PIECE_OFF, tok_c * (-slope2))
        qt_a[0, hd, 0] = jnp.where(is_q, zq[pair, :] * c_hd,
                                   jnp.where(between(off, ONE_OFF, N_PIECE), 1.0,
                                             jnp.where(between(off, PIECE_OFF, N_PIECE), ext, 0.0))).astype(BF16)
        vt_a[0, hd, 0] = _vt_with_ones(zv[pair, :], hd)

    zcf = _dot(h, wnn_ref[:, NN_CKV:NN_KR])
    in_f = (lane & (HEAD_DIM - 1)) < FG_W
    lf = jnp.where(in_f, _log_sigmoid(zcf[:, LANES:] + fb_ref[...]), 0.0)
    p0, p1, p2 = _split3(lf)
    tri = tri_ref[...]
    dec = _dot(tri, p0) + _dot(tri, p1) + _dot(tri, p2) + carry[0:1, :]
    carry[0:1, :] = dec[tm - 1:tm, :]
    e_k = _pieces_by(pq_ref[...], dec * (-LOG2E))
    e_q = _pieces_by(pk_ref[...], dec.T * LOG2E)
    zk = _dot(h, wnn_ref[:, NN_BK:NN_D])
    zq = _dot_nt(wnt_ref[NT_BQ:NT_BV, :], h)
    zv = _dot_nt(wnt_ref[NT_BV:NT_END, :], h)
    for hd in range(N_GROUP):
        pair = head_sl(hd // 2)
        is_k, off = feature_masks(lane, hd)
        kr_b[0, hd] = jnp.where(is_k, zk[:, pair],
                                jnp.where(between(off, FGA_OFF, FG_W // 2), e_k,
                                          jnp.where(between(off, FGB_OFF + N_PIECE * hd, N_PIECE), 1.0, 0.0))
                                ).astype(BF16)
        is_q, off = feature_masks(rowi, hd)
        qt_b[0, hd, 0] = jnp.where(is_q, zq[pair, :] * c_hd,
                                   jnp.where(between(off, FGB_OFF, FG_W // 2), e_q,
                                             jnp.where(between(off, FGA_OFF + N_PIECE * hd, N_PIECE), 1.0, 0.0))
                                   ).astype(BF16)
        vt_b[0, hd, 0] = _vt_with_ones(zv[pair, :], hd)

    c_mla = (HEAD_DIM + MLA_ROPE) ** -0.5 * LOG2E
    cqn = _rms(_dot(h, wnn_ref[:, NN_CQ:NN_CKV]), qn_ref[...]).astype(BF16)
    ckvn = _rms(zcf[:, :LANES], kvn_ref[...]).astype(BF16)
    zr = _dot(h, wnn_ref[:, NN_KR:NN_END])
    rot_k = zr[:, :LANES] * cosk_ref[...] + zr[:, LANES:] * sink_ref[...]
    kn = _dot(ckvn, wkn_ref[...])
    qt = _dot_nt(wuqt_ref[...], cqn)
    qrht = _dot_nt(wuqrht_ref[...], cqn)
    vt = _dot_nt(wvt_ref[...], ckvn)
    cosq, sinq = cosq_ref[...], sinq_ref[...]
    for hd in range(N_GROUP):
        kr_c[0, hd] = (kn[:, head_sl(hd)] + rot_k).astype(BF16)
        qt_c[0, hd, 0] = ((qt[head_sl(hd), :] * cosq + qrht[head_sl(hd), :] * sinq) * c_mla).astype(BF16)
        vt_c[0, hd, 0] = _vt_with_ones(vt[head_sl(hd // 2), :], hd)

    yd_ref[0] = _dot(h, wnn_ref[:, NN_D:NN_CQ])


def _proj_call(x3, g, wnn, wnt, fb, pq, pk, tri, qn, kvn, wuqt, wuqrht, wkn, wvt, cosq, sinq, cosk, sink):
    b, t, _ = x3.shape
    tm = TILE
    nk = t // tm
    per_q = ATT_TQ // tm
    kr_shape = jax.ShapeDtypeStruct((b, N_GROUP, t, LANES), BF16)
    vt_shape = jax.ShapeDtypeStruct((b, N_GROUP, nk, LANES, tm), BF16)
    qt_shape = jax.ShapeDtypeStruct((b, N_GROUP, t // ATT_TQ, LANES, ATT_TQ), BF16)
    kr_spec = pl.BlockSpec((1, N_GROUP, tm, LANES), lambda bi, i: (bi, 0, i, 0))
    vt_spec = pl.BlockSpec((1, N_GROUP, 1, LANES, tm), lambda bi, i: (bi, 0, i, 0, 0))
    qt_spec = pl.BlockSpec((1, N_GROUP, 1, LANES, tm), lambda bi, i: (bi, 0, i // per_q, 0, i % per_q))
    in_specs = [
        pl.BlockSpec((1, tm, D_MODEL), lambda bi, i: (bi, i, 0)),
        _const_spec((1, D_MODEL)),
        _const_spec((D_MODEL, NN_END)),
        _const_spec((NT_END, D_MODEL)),
        _const_spec((1, LANES)),
        _const_spec((1, LANES)),
        _const_spec((LANES, 1)),
        _const_spec((tm, tm)),
        _const_spec((1, MLA_Q_RANK)),
        _const_spec((1, MLA_KV_RANK)),
        _const_spec((N_GROUP * LANES, MLA_Q_RANK)),
        _const_spec((N_GROUP * LANES, MLA_Q_RANK)),
        _const_spec((MLA_KV_RANK, N_GROUP * LANES)),
        _const_spec((HW, MLA_KV_RANK)),
        pl.BlockSpec((LANES, tm), lambda bi, i: (0, i)),
        pl.BlockSpec((LANES, tm), lambda bi, i: (0, i)),
        pl.BlockSpec((tm, LANES), lambda bi, i: (i, 0)),
        pl.BlockSpec((tm, LANES), lambda bi, i: (i, 0)),
    ]
    out_shape = [kr_shape, qt_shape, vt_shape] * 3 + [jax.ShapeDtypeStruct((b, t, 3 * HW), F32)]
    out_specs = [kr_spec, qt_spec, vt_spec] * 3 + [pl.BlockSpec((1, tm, 3 * HW), lambda bi, i: (bi, i, 0))]
    return pl.pallas_call(
        functools.partial(_proj_kernel, tm=tm),
        out_shape=out_shape,
        grid=(b, nk),
        in_specs=in_specs,
        out_specs=out_specs,
        scratch_shapes=[pltpu.VMEM((8, LANES), F32)],
        compiler_params=pltpu.CompilerParams(dimension_semantics=("arbitrary", "arbitrary"),
                                             vmem_limit_bytes=VMEM_LIMIT),
        name="proj",
    )(x3, g, wnn, wnt, fb, pq, pk, tri, qn, kvn, wuqt, wuqrht, wkn, wvt, cosq, sinq, cosk, sink)


def _gate_kernel(qt0_ref, kr_ref, qt_ref, *, t, tq):
    nq = t // tq
    nblk = t // MOBA_BLOCK
    blk = lax.broadcasted_iota(jnp.int32, (nblk, tq), 0)
    lane = lax.broadcasted_iota(jnp.int32, (nblk, LANES), 1)
    for hd in range(N_GROUP):
        ksum = jnp.zeros((nblk, LANES), F32)
        for j in range(nq):
            key_blk = (j * tq + lax.broadcasted_iota(jnp.int32, (nblk, tq), 1)) // MOBA_BLOCK
            ind = jnp.where(key_blk == blk, 1.0, 0.0).astype(BF16)
            ksum = ksum + _dot(ind, kr_ref[0, hd, j * tq:(j + 1) * tq, :])
        qk0 = HEAD_DIM * (hd % 2)
        sel0 = HEAD_DIM - qk0 + SEL_OFF
        is_k = (lane >= qk0) & (lane < qk0 + HEAD_DIM)
        kmean = jnp.where(is_k, ksum * (1.0 / MOBA_BLOCK), 0.0).astype(BF16)
        for qi in range(nq):
            q0 = qt0_ref[0, hd, qi]
            cur = (qi * tq + lax.broadcasted_iota(jnp.int32, (nblk, tq), 1)) // MOBA_BLOCK
            valid = blk < cur
            gate = jnp.where(valid, _dot(kmean, q0), -jnp.inf)
            rank = jnp.zeros((nblk, tq), F32)
            for m in range(nblk):
                gm = gate[m:m + 1, :]
                ahead = (gm > gate) | ((gm == gate) & (blk > m))
                rank = rank + jnp.where(ahead, 1.0, 0.0)
            keep = (valid & (rank < MOBA_TOPK)) | (blk == cur)
            selb = jnp.where(keep, 0.0, NEG).astype(BF16)
            parts = ([q0[:sel0]] if sel0 else []) + [selb, q0[sel0 + nblk:]]
            qt_ref[0, hd, qi] = jnp.concatenate(parts, axis=0)


def _gate_call(qt0, kr):
    b, _, nq, _, tq = qt0.shape
    t = nq * tq
    nblk = t // MOBA_BLOCK
    assert nblk == MOBA_NBLK and SEL_OFF % 16 == 0
    qt_spec = pl.BlockSpec((1, N_GROUP, nq, LANES, tq), lambda bi: (bi, 0, 0, 0, 0))
    kr_spec = pl.BlockSpec((1, N_GROUP, t, LANES), lambda bi: (bi, 0, 0, 0))
    return pl.pallas_call(
        functools.partial(_gate_kernel, t=t, tq=tq),
        out_shape=jax.ShapeDtypeStruct(qt0.shape, BF16),
        grid=(b,),
        in_specs=[qt_spec, kr_spec],
        out_specs=qt_spec,
        compiler_params=pltpu.CompilerParams(dimension_semantics=("arbitrary",),
                                             vmem_limit_bytes=VMEM_LIMIT),
        name="moba_gate",
    )(qt0, kr)


def _attn_kernel(kr_ref, qt_ref, vt_ref, o_ref, acc_ref, *, tq, tk):
    qi = pl.program_id(1)
    ratio = tq // tk
    rowi = lax.broadcasted_iota(jnp.int32, (LANES, tq), 0)
    acc_ref[...] = jnp.zeros(acc_ref.shape, F32)

    def step(j, ms, c0, masked):
        w = tq - c0
        k0 = pl.multiple_of(j * tk, tk)
        sts = [_dot(kr_ref[0, hd, pl.ds(k0, tk), :], qt_ref[0, hd, 0, :, c0:]) for hd in range(N_GROUP)]
        out = []
        for hd in range(N_GROUP):
            st = sts[hd]
            if masked:
                keyi = lax.broadcasted_iota(jnp.int32, (tk, w), 0)
                qryi = lax.broadcasted_iota(jnp.int32, (tk, w), 1)
                st = jnp.where(keyi <= qryi, st, NEG)
            m_old = ms[hd][:, c0:]
            m_new = jnp.maximum(m_old, jnp.max(st, axis=0, keepdims=True))
            p = jnp.exp2(st - m_new)
            pv = _dot(vt_ref[0, hd, j], p.astype(BF16))
            acc_ref[hd, :, c0:] = jnp.exp2(m_old - m_new) * acc_ref[hd, :, c0:] + pv
            out.append(m_new if c0 == 0 else jnp.concatenate([ms[hd][:, :c0], m_new], axis=1))
        return tuple(out)

    def lagged_step(j, carry):
        used, nxt, excess = carry
        k0 = pl.multiple_of(j * tk, tk)
        sts = [_dot(kr_ref[0, hd, pl.ds(k0, tk), :], qt_ref[0, hd, 0]) for hd in range(N_GROUP)]
        new_nxt = []
        for hd in range(N_GROUP):
            st, ref = sts[hd], nxt[hd]
            p = jnp.exp2(st - ref)
            block_max = jnp.max(st, axis=0, keepdims=True)
            pv = _dot(vt_ref[0, hd, j], p.astype(BF16))
            acc_ref[hd] = jnp.exp2(used[hd] - ref) * acc_ref[hd] + pv
            excess = jnp.maximum(excess, block_max - ref)
            new_nxt.append(jnp.maximum(ref, block_max))
        return nxt, tuple(new_nxt), excess

    def diagonal(ms):
        for d in range(ratio):
            ms = step(qi * ratio + d, ms, d * tk, True)
        return ms

    m0 = tuple(jnp.full((1, tq), -jnp.inf, F32) for _ in range(N_GROUP))
    ms = diagonal(m0)
    _, _, excess = lax.fori_loop(0, qi * ratio, lagged_step, (ms, ms, jnp.full((1, tq), -jnp.inf, F32)))

    @pl.when(jnp.max(excess) > LAG_LIMIT)
    def _():
        acc_ref[...] = jnp.zeros(acc_ref.shape, F32)
        diagonal(lax.fori_loop(0, qi * ratio, lambda j, c: step(j, c, 0, False), m0))

    for pair in range(N_GROUP // 2):
        even, odd = acc_ref[2 * pair], acc_ref[2 * pair + 1]
        out_t = jnp.where(rowi < HEAD_DIM, even / even[HEAD_DIM:HEAD_DIM + 1, :], odd / odd[0:1, :])
        o_ref[0, :, pair * LANES:(pair + 1) * LANES] = out_t.T.astype(BF16)


def _attn_call(kr, qt, vt, name):
    b, _, t, _ = kr.shape
    tq, tk = qt.shape[-1], vt.shape[-1]
    return pl.pallas_call(
        functools.partial(_attn_kernel, tq=tq, tk=tk),
        out_shape=jax.ShapeDtypeStruct((b, t, HW), BF16),
        grid=(b, t // tq),
        in_specs=[
            pl.BlockSpec((1, N_GROUP, t, LANES), lambda bi, i: (bi, 0, 0, 0)),
            pl.BlockSpec((1, N_GROUP, 1, LANES, tq), lambda bi, i: (bi, 0, i, 0, 0)),
            pl.BlockSpec((1, N_GROUP, t // tk, LANES, tk), lambda bi, i: (bi, 0, 0, 0, 0)),
        ],
        out_specs=pl.BlockSpec((1, tq, HW), lambda bi, i: (bi, i, 0)),
        scratch_shapes=[pltpu.VMEM((N_GROUP, LANES, tq), F32)],
        compiler_params=pltpu.CompilerParams(dimension_semantics=("arbitrary", "arbitrary"),
                                             vmem_limit_bytes=VMEM_LIMIT),
        name=name,
    )(kr, qt, vt)


DIL_UNROLL = 16


def _dil_kernel(q_ref, k_ref, v_ref, bias_ref, bias2_ref, o_ref, acc_ref, max_ref, *, t):
    span = DIL_SPAN
    c_hd = HEAD_DIM ** -0.5 * LOG2E

    def rows(start, size, dil):
        return pl.ds(start, size) if dil == 1 else pl.ds(start, size, stride=dil)

    def own(shape, hh):
        lane = lax.broadcasted_iota(jnp.int32, shape, 1)
        return (lane < HEAD_DIM) if hh == 0 else (lane >= HEAD_DIM)

    def finish_blocks(work, br, dil):
        for hh, q0, nq, v2, bias, qk in work:
            va = jnp.where(own(v2.shape, hh), v2, 1.0).astype(BF16)
            s = qk + bias
            m = jnp.max(s, axis=1, keepdims=True)
            p = jnp.exp2(s - m)
            acc_ref[br, hh, rows(q0, nq, dil), :] = _dot(p.astype(BF16), va)
            max_ref[br, hh, rows(q0, nq, dil), :] = jnp.broadcast_to(m, (nq, LANES))

    for br, (window, dil) in enumerate(DIL_PAIRS):
        nb = t // dil // span

        def group(it, carry, br=br, dil=dil, nb=nb):
            work = []
            for u in range(DIL_UNROLL):
                idx = it * DIL_UNROLL + u
                r, n = idx // nb, idx % nb
                q0 = r + n * (span * dil)
                first = 1 - jnp.minimum(n, 1)
                k0 = q0 - (1 - first) * (span * dil)
                q2 = q_ref[0, rows(q0, span, dil), :] * c_hd
                k2 = k_ref[0, rows(k0, 2 * span, dil), :].astype(BF16)
                v2 = v_ref[0, rows(k0, 2 * span, dil), :]
                for hh in range(2):
                    qz = jnp.where(own(q2.shape, hh), q2, 0.0).astype(BF16)
                    work.append((hh, q0, span, v2, bias_ref[hh, 2 * br + first], _dot_nt(qz, k2)))
            finish_blocks(work, br, dil)
            return carry

        def group_whole(it, carry, br=br, dil=dil):
            work = []
            for u in range(DIL_UNROLL // 2):
                r = it * (DIL_UNROLL // 2) + u
                q2 = q_ref[0, rows(r, 2 * span, dil), :] * c_hd
                k2 = k_ref[0, rows(r, 2 * span, dil), :].astype(BF16)
                v2 = v_ref[0, rows(r, 2 * span, dil), :]
                for hh in range(2):
                    qz = jnp.where(own(q2.shape, hh), q2, 0.0).astype(BF16)
                    work.append((hh, r, 2 * span, v2, bias2_ref[hh], _dot_nt(qz, k2)))
            finish_blocks(work, br, dil)
            return carry

        if nb == 2:
            lax.fori_loop(0, dil // (DIL_UNROLL // 2), group_whole, 0)
        else:
            lax.fori_loop(0, dil * nb // DIL_UNROLL, group, 0)

    chunk = 256
    lane_c = lax.broadcasted_iota(jnp.int32, (chunk, LANES), 1)

    def finish(ci, carry):
        sl = pl.ds(ci * chunk, chunk)
        outs = []
        for hh in range(2):
            ms = [max_ref[br, hh, sl, :] for br in range(len(DIL_PAIRS))]
            mx = functools.reduce(jnp.maximum, ms)
            tot = sum(acc_ref[br, hh, sl, :] * jnp.exp2(m - mx) for br, m in enumerate(ms))
            outs.append(tot / pltpu.roll(tot, HEAD_DIM, axis=1))
        o_ref[0, sl, :] = jnp.where(lane_c < HEAD_DIM, outs[0], outs[1]).astype(BF16)
        return carry

    lax.fori_loop(0, t // chunk, finish, 0)


def _dil_bias_table():
    qi = np.arange(DIL_SPAN)[:, None]
    kj = np.arange(2 * DIL_SPAN)[None, :]
    tab = np.zeros((N_GROUP, 2 * len(DIL_PAIRS), DIL_SPAN, 2 * DIL_SPAN), np.float32)
    for variant in range(2):
        delta = qi - kj + (DIL_SPAN if variant == 0 else 0)
        valid = (delta >= 0) & (delta <= DIL_SPAN)
        for hd in range(N_GROUP):
            for br, (_, dil) in enumerate(DIL_PAIRS):
                tab[hd, 2 * br + variant] = np.where(valid, -SLOPES_DIL[hd] * LOG2E * (delta * dil), NEG)
    return tab


def _dil_bias_whole(dil):
    delta = np.arange(2 * DIL_SPAN)[:, None] - np.arange(2 * DIL_SPAN)[None, :]
    valid = (delta >= 0) & (delta <= DIL_SPAN)
    return np.stack([np.where(valid, -SLOPES_DIL[hd] * LOG2E * (delta * dil), NEG)
                     for hd in range(N_GROUP)]).astype(np.float32)


def _dil_call(yd, bias, bias2):
    b, t, _ = yd.shape
    for window, dil in DIL_PAIRS:
        nb = t // dil // DIL_SPAN
        assert window // dil == DIL_SPAN and nb * dil * DIL_SPAN == t and nb >= 2
        assert (dil * nb) % DIL_UNROLL == 0
    n_pair = N_GROUP // 2
    n_br = len(DIL_PAIRS)
    return pl.pallas_call(
        functools.partial(_dil_kernel, t=t),
        out_shape=jax.ShapeDtypeStruct((b, t, HW), BF16),
        grid=(b, n_pair),
        in_specs=[
            pl.BlockSpec((1, t, LANES), lambda bi, p: (bi, 0, p)),
            pl.BlockSpec((1, t, LANES), lambda bi, p: (bi, 0, n_pair + p)),
            pl.BlockSpec((1, t, LANES), lambda bi, p: (bi, 0, 2 * n_pair + p)),
            pl.BlockSpec((2, 2 * n_br, DIL_SPAN, 2 * DIL_SPAN), lambda bi, p: (p, 0, 0, 0)),
            pl.BlockSpec((2, 2 * DIL_SPAN, 2 * DIL_SPAN), lambda bi, p: (p, 0, 0)),
        ],
        out_specs=pl.BlockSpec((1, t, LANES), lambda bi, p: (bi, 0, p)),
        scratch_shapes=[pltpu.VMEM((n_br, 2, t, LANES), F32), pltpu.VMEM((n_br, 2, t, LANES), F32)],
        compiler_params=pltpu.CompilerParams(dimension_semantics=("arbitrary", "arbitrary"),
                                             vmem_limit_bytes=VMEM_LIMIT),
        name="dilated",
    )(yd, yd, yd, bias, bias2)


def _cols(w, spec):
    parts = []
    for item in spec:
        if isinstance(item, int):
            parts.append(jnp.zeros((w.shape[0], item), w.dtype))
        else:
            piece = w[:, item[0]:item[0] + item[1]]
            parts.append(-piece if len(item) == 3 else piece)
    return jnp.concatenate(parts, axis=1)


def _layout_specs():
    half = MLA_ROPE // 2
    pad = LANES - HEAD_DIM
    forget = [(BF0 + hd, 1) for hd in range(N_GROUP) for _ in range(N_PIECE)]
    forget_half = forget + forget + [HEAD_DIM - FG_W]
    rope_pad = LANES - HEAD_DIM - MLA_ROPE
    nn = ([(AK0, HW), (BK0, HW), (DQ0, 3 * HW), (CQ0, MLA_Q_RANK), (CKV0, MLA_KV_RANK)]
          + forget_half + forget_half
          + [HEAD_DIM, (CKR0, MLA_ROPE), rope_pad]
          + [HEAD_DIM, (CKR0 + half, half, -1.0), (CKR0, half), rope_pad])
    nt = [(AQ0, HW), (AV0, HW), (BQ0, HW), (BV0, HW)]
    qw = HEAD_DIM + MLA_ROPE
    uq = [s for hd in range(N_GROUP) for s in ((hd * qw, qw), LANES - qw)]
    uqrh = [s for hd in range(N_GROUP)
            for s in (HEAD_DIM, (hd * qw + HEAD_DIM + half, half, -1.0), (hd * qw + HEAD_DIM, half), LANES - qw)]
    kn = [s for hd in range(N_GROUP) for s in ((hd * LANES, HEAD_DIM), pad)]
    vv = [(hd * LANES + HEAD_DIM, HEAD_DIM) for hd in range(N_GROUP)]
    return nn, nt, uq, uqrh, kn, vv


_SPECS = _layout_specs()


def _rope_tables(t):
    inv = ROPE_THETA ** (-jnp.arange(0, MLA_ROPE, 2, dtype=F32) / MLA_ROPE)
    ang = jnp.arange(t, dtype=F32)[:, None] * inv[None, :]
    cos, sin = jnp.cos(ang), jnp.sin(ang)
    z_lo = jnp.zeros((t, HEAD_DIM), F32)
    z_hi = jnp.zeros((t, LANES - HEAD_DIM - MLA_ROPE), F32)
    cosq = jnp.concatenate([jnp.ones((t, HEAD_DIM), F32), cos, cos, z_hi], axis=1)
    sin_tab = jnp.concatenate([z_lo, sin, sin, z_hi], axis=1)
    cosk = jnp.concatenate([z_lo, cos, cos, z_hi], axis=1)
    return cosq.T, sin_tab.T, cosk, sin_tab


def kernel(x, norm_ffn1, ffn1_gate, ffn1_up, ffn1_down, norm_mix, w_in, forget_bias, mla_q_norm,
           mla_kv_norm, w_uq, w_ukv, w_out, norm_ffn2, ffn2_gate, ffn2_up, ffn2_down, norm_final):
    b, t, d = x.shape
    depth = w_in.shape[0]
    assert d == D_MODEL and t % ATT_TQ == 0 and ATT_TQ % TILE == 0 and (b * t) % FFN_TILE == 0
    nn_spec, nt_spec, uq_spec, uqrh_spec, kn_spec, vv_spec = _SPECS
    cosq, sinq, cosk, sink = _rope_tables(t)
    tri = jnp.tril(jnp.ones((TILE, TILE), F32)).astype(BF16)
    dil_bias = jnp.asarray(_dil_bias_table())
    whole = [dil for _, dil in DIL_PAIRS if t // dil // DIL_SPAN == 2]
    dil_bias2 = jnp.asarray(_dil_bias_whole(whole[0] if whole else 1))
    lane = np.arange(LANES)
    fg_off = lane % HEAD_DIM
    in_f = fg_off < FG_W
    fg_head = np.where(in_f, (fg_off // N_PIECE) % N_GROUP, 0)
    fg_piece = np.where(in_f, fg_off % N_PIECE, -1).astype(np.int32)
    pq, pk = jnp.asarray(fg_piece[None, :]), jnp.asarray(fg_piece[:, None])
    bf = lambda w: w.astype(BF16)
    row = lambda v: v.reshape(1, -1)

    xf = x.reshape(b * t, d)
    for l in range(depth):
        xf = _ffn_call(xf, None, None, row(norm_ffn1[l]), ffn1_gate, ffn1_up, ffn1_down, l, None)
        w_l = bf(w_in[l])
        wnn = _cols(w_l, nn_spec)
        wnt = _cols(w_l, nt_spec).T
        fb = jnp.where(jnp.asarray(in_f), jnp.take(forget_bias[l], jnp.asarray(fg_head)), 0.0)[None, :]
        wuq_l, wukv_l = bf(w_uq[l]), bf(w_ukv[l])
        wuqt = _cols(wuq_l, uq_spec).T
        wuqrht = _cols(wuq_l, uqrh_spec).T
        wkn = _cols(wukv_l, kn_spec)
        wvt = _cols(wukv_l, vv_spec).T
        (kr_a, qt_a0, vt_a, kr_b, qt_b, vt_b, kr_c, qt_c, vt_c, yd) = _proj_call(
            xf.reshape(b, t, d), row(norm_mix[l]), wnn, wnt, fb, pq, pk, tri,
            row(mla_q_norm[l]), row(mla_kv_norm[l]), wuqt, wuqrht, wkn, wvt, cosq, sinq, cosk, sink)
        qt_a = _gate_call(qt_a0, kr_a)
        o_a = _attn_call(kr_a, qt_a, vt_a, "attn_moba")
        o_b = _attn_call(kr_b, qt_b, vt_b, "attn_forget")
        o_c = _attn_call(kr_c, qt_c, vt_c, "attn_mla")
        o_d = _dil_call(yd, dil_bias, dil_bias2)
        mix = [o.reshape(b * t, HW) for o in (o_a, o_b, o_c, o_d)]
        xf = _ffn_call(xf, mix, bf(w_out[l]), row(norm_ffn2[l]), ffn2_gate, ffn2_up, ffn2_down, l,
                       row(norm_final) if l == depth - 1 else None)
    return xf.reshape(b, t, d)
```

```python
import functools

import numpy as np
import jax
import jax.numpy as jnp
from jax import lax
from jax.experimental import pallas as pl
from jax.experimental.pallas import tpu as pltpu

F32 = jnp.float32
BF16 = jnp.bfloat16

D_MODEL = 1024
HEAD_DIM = 64
N_GROUP = 4
HW = N_GROUP * HEAD_DIM
LANES = 128
MOBA_BLOCK = 256
MOBA_TOPK = 3
MLA_Q_RANK = 256
MLA_KV_RANK = 128
MLA_ROPE = 32
ROPE_THETA = 10000.0
DIL_PAIRS = ((128, 1), (512, 4), (2048, 16))
DIL_SPAN = 128
D_FF = 2816
RMS_EPS = 1e-6
NEG = -1e30
LOG2E = 1.4426950408889634

N_PIECE = 3
MOBA_NBLK = 16
SEL_OFF = 0
ONE_OFF = 16
PIECE_OFF = 19
FGA_OFF = 0
FGB_OFF = N_GROUP * N_PIECE
FG_W = 2 * N_GROUP * N_PIECE

TILE = 512
ATT_TQ = 1024
LAG_LIMIT = 64.0
FFN_TILE = 512
FF_CHUNK = 256
VMEM_LIMIT = 56 * 1024 * 1024

_ALIBI = [2.0 ** (-8.0 * (i + 1) / 8) for i in range(8)]
SLOPES_MOBA = _ALIBI[1::2]
SLOPES_DIL = _ALIBI[0::2]

_WIDTHS = (HW, HW, HW, HW, HW, HW, N_GROUP, MLA_Q_RANK, MLA_KV_RANK, MLA_ROPE, HW, HW, HW)
_OFF = np.concatenate([[0], np.cumsum(_WIDTHS)]).astype(np.int64)
(AQ0, AK0, AV0, BQ0, BK0, BV0, BF0, CQ0, CKV0, CKR0, DQ0, DK0, DV0, IN_WIDTH) = [int(v) for v in _OFF]


def _dot(a, b):
    return jnp.dot(a, b, preferred_element_type=F32)


def _dot_nt(a, b):
    return lax.dot_general(a, b, (((1,), (1,)), ((), ())), preferred_element_type=F32)


def _rms(x, g):
    return x * lax.rsqrt(jnp.mean(x * x, axis=-1, keepdims=True) + RMS_EPS) * g


def _split3(x):
    p0 = x.astype(BF16)
    r = x - p0.astype(F32)
    p1 = r.astype(BF16)
    p2 = (r - p1.astype(F32)).astype(BF16)
    return p0, p1, p2


def _log_sigmoid(x):
    return jnp.minimum(x, 0.0) - jnp.log(1.0 + jnp.exp(-jnp.abs(x)))


def _ffn_kernel(*refs, has_mix, final_norm):
    it = iter(refs)
    x_ref = next(it)
    if has_mix:
        o_refs = [next(it) for _ in range(4)]
        wo_ref = next(it)
    g_ref, wg_ref, wu_ref, wd_ref = next(it), next(it), next(it), next(it)
    gf_ref = next(it) if final_norm else None
    out_ref = next(it)

    x = x_ref[...]
    if has_mix:
        for i, o_ref in enumerate(o_refs):
            x = x + _dot(o_ref[...], wo_ref[i * HW:(i + 1) * HW, :])
    h = _rms(x, g_ref[...]).astype(BF16)
    y = jnp.zeros(x.shape, F32)
    for c in range(D_FF // FF_CHUNK):
        sl = slice(c * FF_CHUNK, (c + 1) * FF_CHUNK)
        gate = _dot(h, wg_ref[:, sl].astype(BF16))
        up = _dot(h, wu_ref[:, sl].astype(BF16))
        act = (gate * jax.nn.sigmoid(gate) * up).astype(BF16)
        y = y + _dot(act, wd_ref[sl, :].astype(BF16))
    x = x + 0.5 * y
    if final_norm:
        x = _rms(x, gf_ref[...])
    out_ref[...] = x


def _const_spec(shape):
    nd = len(shape)
    return pl.BlockSpec(shape, lambda *_: (0,) * nd, pipeline_mode=pl.Buffered(1))


def _layer_spec(shape, layer):
    nd = len(shape)
    return pl.BlockSpec((None,) + tuple(shape), lambda *_: (layer,) + (0,) * nd, pipeline_mode=pl.Buffered(1))


def _ffn_call(x, mix, w_out, g, wg, wu, wd, layer, g_final):
    n = x.shape[0]
    tm = FFN_TILE
    has_mix = mix is not None
    final_norm = g_final is not None
    row = lambda i: (i, 0)
    args, specs = [x], [pl.BlockSpec((tm, D_MODEL), row)]
    if has_mix:
        for o in mix:
            args.append(o)
            specs.append(pl.BlockSpec((tm, HW), row))
        args.append(w_out)
        specs.append(_const_spec((D_MODEL, D_MODEL)))
    args += [g, wg, wu, wd]
    specs += [_const_spec((1, D_MODEL)), _layer_spec((D_MODEL, D_FF), layer), _layer_spec((D_MODEL, D_FF), layer),
              _layer_spec((D_FF, D_MODEL), layer)]
    if final_norm:
        args.append(g_final)
        specs.append(_const_spec((1, D_MODEL)))
    return pl.pallas_call(
        functools.partial(_ffn_kernel, has_mix=has_mix, final_norm=final_norm),
        out_shape=jax.ShapeDtypeStruct((n, D_MODEL), F32),
        grid=(n // tm,),
        in_specs=specs,
        out_specs=pl.BlockSpec((tm, D_MODEL), row),
        compiler_params=pltpu.CompilerParams(dimension_semantics=("arbitrary",),
                                             vmem_limit_bytes=VMEM_LIMIT),
        name="ffn_mix" if has_mix else "ffn",
    )(*args)


PAD_F = -(BF0 + N_GROUP) % LANES
PAD_R = -(CKR0 + MLA_ROPE + PAD_F) % LANES
AL_AK, AL_BK, AL_CQ = AK0, BK0, CQ0 + PAD_F
AL_CKV = CKV0 + PAD_F
AL_D = DQ0 + PAD_F + PAD_R
AL_END = IN_WIDTH + PAD_F + PAD_R
assert all(v % LANES == 0 for v in (AL_AK, AL_BK, AL_CQ, AL_CKV, AL_D, AL_END)) and AL_D - AL_CKV == 2 * LANES
WX_F, WX_KR, WX_KRH, WX_END = 0, LANES, 2 * LANES, 3 * LANES
NT_AQ, NT_AV, NT_BQ, NT_BV, NT_END = AQ0, AV0, BQ0, BV0, BF0


def _pieces_select(idx, base, val):
    p0, p1, p2 = _split3(val)
    return jnp.where(idx == base, p0.astype(F32),
                     jnp.where(idx == base + 1, p1.astype(F32), p2.astype(F32)))


def _pieces_by(which, val):
    p0, p1, p2 = _split3(val)
    return jnp.where(which == 0, p0.astype(F32), jnp.where(which == 1, p1.astype(F32), p2.astype(F32)))


def _vt_with_ones(vt, head):
    row = lax.broadcasted_iota(jnp.int32, vt.shape, 0)
    keep = (row < HEAD_DIM) if head % 2 == 0 else (row >= HEAD_DIM)
    return jnp.where(keep, vt, 1.0).astype(BF16)


def _proj_kernel(x_ref, g_ref, wal_ref, wx_ref, wnt_ref, fb_ref, pq_ref, pk_ref, tri_ref,
                 qn_ref, kvn_ref, wuqt_ref, wuqrht_ref, wkn_ref, wvt_ref,
                 cosq_ref, sinq_ref, cosk_ref, sink_ref,
                 kr_a, qt_a, vt_a, kr_b, qt_b, vt_b, kr_c, qt_c, vt_c, yd_ref,
                 carry, *, tm):
    i = pl.program_id(1)
    h = _rms(x_ref[0], g_ref[...]).astype(BF16)
    lane = lax.broadcasted_iota(jnp.int32, (tm, LANES), 1)
    rowi = lax.broadcasted_iota(jnp.int32, (LANES, tm), 0)
    tok_r_i = i * tm + lax.broadcasted_iota(jnp.int32, (tm, LANES), 0)
    tok_r = tok_r_i.astype(F32)
    tok_c = (i * tm + lax.broadcasted_iota(jnp.int32, (LANES, tm), 1)).astype(F32)
    c_hd = HEAD_DIM ** -0.5 * LOG2E
    head_sl = lambda hd: slice(hd * LANES, (hd + 1) * LANES)

    @pl.when(i == 0)
    def _():
        carry[...] = jnp.zeros_like(carry)

    def feature_masks(idx, hd):
        qk0 = HEAD_DIM * (hd % 2)
        return (idx >= qk0) & (idx < qk0 + HEAD_DIM), idx - (HEAD_DIM - qk0)

    between = lambda off, lo, n: (off >= lo) & (off < lo + n)

    zk = _dot(h, wal_ref[:, AL_AK:AL_AK + HW])
    zq = _dot_nt(wnt_ref[NT_AQ:NT_AQ + HW, :], h)
    zv = _dot_nt(wnt_ref[NT_AV:NT_AV + HW, :], h)
    for hd in range(N_GROUP):
        slope2 = SLOPES_MOBA[hd] * LOG2E
        pair = head_sl(hd // 2)
        is_k, off = feature_masks(lane, hd)
        ind = between(off, SEL_OFF, MOBA_NBLK) & ((tok_r_i // MOBA_BLOCK) == off - SEL_OFF)
        ext = _pieces_select(off, ONE_OFF, tok_r * slope2)
        kr_a[0, hd] = jnp.where(is_k, zk[:, pair],
                                jnp.where(between(off, ONE_OFF, N_PIECE), ext,
                                          jnp.where(between(off, PIECE_OFF, N_PIECE) | ind, 1.0, 0.0))).astype(BF16)
        is_q, off = feature_masks(rowi, hd)
        ext = _pieces_select(off, PIECE_OFF, tok_c * (-slope2))
        qt_a[0, hd, 0] = jnp.where(is_q, zq[pair, :] * c_hd,
                                   jnp.where(between(off, ONE_OFF, N_PIECE), 1.0,
                                             jnp.where(between(off, PIECE_OFF, N_PIECE), ext, 0.0))).astype(BF16)
        vt_a[0, hd, 0] = _vt_with_ones(zv[pair, :], hd)

    zfr = _dot(h, wx_ref[:, WX_F:WX_KRH])
    in_f = (lane & (HEAD_DIM - 1)) < FG_W
    lf = jnp.where(in_f, _log_sigmoid(zfr[:, :LANES] + fb_ref[...]), 0.0)
    p0, p1, p2 = _split3(lf)
    tri = tri_ref[...]
    dec = _dot(tri, p0) + _dot(tri, p1) + _dot(tri, p2) + carry[0:1, :]
    carry[0:1, :] = dec[tm - 1:tm, :]
    e_k = _pieces_by(pq_ref[...], dec * (-LOG2E))
    e_q = _pieces_by(pk_ref[...], dec.T * LOG2E)
    zk = _dot(h, wal_ref[:, AL_BK:AL_BK + HW])
    zq = _dot_nt(wnt_ref[NT_BQ:NT_BQ + HW, :], h)
    zv = _dot_nt(wnt_ref[NT_BV:NT_BV + HW, :], h)
    for hd in range(N_GROUP):
        pair = head_sl(hd // 2)
        is_k, off = feature_masks(lane, hd)
        kr_b[0, hd] = jnp.where(is_k, zk[:, pair],
                                jnp.where(between(off, FGA_OFF, FG_W // 2), e_k,
                                          jnp.where(between(off, FGB_OFF + N_PIECE * hd, N_PIECE), 1.0, 0.0))
                                ).astype(BF16)
        is_q, off = feature_masks(rowi, hd)
        qt_b[0, hd, 0] = jnp.where(is_q, zq[pair, :] * c_hd,
                                   jnp.where(between(off, FGB_OFF, FG_W // 2), e_q,
                                             jnp.where(between(off, FGA_OFF + N_PIECE * hd, N_PIECE), 1.0, 0.0))
                                   ).astype(BF16)
        vt_b[0, hd, 0] = _vt_with_ones(zv[pair, :], hd)

    c_mla = (HEAD_DIM + MLA_ROPE) ** -0.5 * LOG2E
    cqn = _rms(_dot(h, wal_ref[:, AL_CQ:AL_CQ + MLA_Q_RANK]), qn_ref[...]).astype(BF16)
    zc = _dot(h, wal_ref[:, AL_CKV:AL_D])
    ckvn = _rms(zc[:, :MLA_KV_RANK], kvn_ref[...]).astype(BF16)
    zrh = _dot(h, wx_ref[:, WX_KRH:WX_END])
    rot_k = zfr[:, LANES:] * cosk_ref[...] + zrh * sink_ref[...]
    kn = _dot(ckvn, wkn_ref[...])
    qt = _dot_nt(wuqt_ref[...], cqn)
    qrht = _dot_nt(wuqrht_ref[...], cqn)
    vt = _dot_nt(wvt_ref[...], ckvn)
    cosq, sinq = cosq_ref[...], sinq_ref[...]
    for hd in range(N_GROUP):
        kr_c[0, hd] = (kn[:, head_sl(hd)] + rot_k).astype(BF16)
        qt_c[0, hd, 0] = ((qt[head_sl(hd), :] * cosq + qrht[head_sl(hd), :] * sinq) * c_mla).astype(BF16)
        vt_c[0, hd, 0] = _vt_with_ones(vt[head_sl(hd // 2), :], hd)

    yd_ref[0] = _dot(h, wal_ref[:, AL_D:AL_END])


def _proj_call(x3, g, wal, wx, wnt, fb, pq, pk, tri, qn, kvn, wuqt, wuqrht, wkn, wvt, cosq, sinq, cosk, sink):
    b, t, _ = x3.shape
    tm = TILE
    nk = t // tm
    per_q = ATT_TQ // tm
    kr_shape = jax.ShapeDtypeStruct((b, N_GROUP, t, LANES), BF16)
    vt_shape = jax.ShapeDtypeStruct((b, N_GROUP, nk, LANES, tm), BF16)
    qt_shape = jax.ShapeDtypeStruct((b, N_GROUP, t // ATT_TQ, LANES, ATT_TQ), BF16)
    kr_spec = pl.BlockSpec((1, N_GROUP, tm, LANES), lambda bi, i: (bi, 0, i, 0))
    vt_spec = pl.BlockSpec((1, N_GROUP, 1, LANES, tm), lambda bi, i: (bi, 0, i, 0, 0))
    qt_spec = pl.BlockSpec((1, N_GROUP, 1, LANES, tm), lambda bi, i: (bi, 0, i // per_q, 0, i % per_q))
    in_specs = [
        pl.BlockSpec((1, tm, D_MODEL), lambda bi, i: (bi, i, 0)),
        _const_spec((1, D_MODEL)),
        _const_spec((D_MODEL, AL_END)),
        _const_spec((D_MODEL, WX_END)),
        _const_spec((NT_END, D_MODEL)),
        _const_spec((1, LANES)),
        _const_spec((1, LANES)),
        _const_spec((LANES, 1)),
        _const_spec((tm, tm)),
        _const_spec((1, MLA_Q_RANK)),
        _const_spec((1, MLA_KV_RANK)),
        _const_spec((N_GROUP * LANES, MLA_Q_RANK)),
        _const_spec((N_GROUP * LANES, MLA_Q_RANK)),
        _const_spec((MLA_KV_RANK, N_GROUP * LANES)),
        _const_spec((HW, MLA_KV_RANK)),
        pl.BlockSpec((LANES, tm), lambda bi, i: (0, i)),
        pl.BlockSpec((LANES, tm), lambda bi, i: (0, i)),
        pl.BlockSpec((tm, LANES), lambda bi, i: (i, 0)),
        pl.BlockSpec((tm, LANES), lambda bi, i: (i, 0)),
    ]
    out_shape = [kr_shape, qt_shape, vt_shape] * 3 + [jax.ShapeDtypeStruct((b, t, 3 * HW), F32)]
    out_specs = [kr_spec, qt_spec, vt_spec] * 3 + [pl.BlockSpec((1, tm, 3 * HW), lambda bi, i: (bi, i, 0))]
    return pl.pallas_call(
        functools.partial(_proj_kernel, tm=tm),
        out_shape=out_shape,
        grid=(b, nk),
        in_specs=in_specs,
        out_specs=out_specs,
        scratch_shapes=[pltpu.VMEM((8, LANES), F32)],
        compiler_params=pltpu.CompilerParams(dimension_semantics=("arbitrary", "arbitrary"),
                                             vmem_limit_bytes=VMEM_LIMIT),
        name="proj",
    )(x3, g, wal, wx, wnt, fb, pq, pk, tri, qn, kvn, wuqt, wuqrht, wkn, wvt, cosq, sinq, cosk, sink)


def _gate_kernel(qt0_ref, kr_ref, qt_ref, *, t, tq):
    nq = t // tq
    nblk = t // MOBA_BLOCK
    blk = lax.broadcasted_iota(jnp.int32, (nblk, tq), 0)
    lane = lax.broadcasted_iota(jnp.int32, (nblk, LANES), 1)
    for hd in range(N_GROUP):
        ksum = jnp.zeros((nblk, LANES), F32)
        for j in range(nq):
            key_blk = (j * tq + lax.broadcasted_iota(jnp.int32, (nblk, tq), 1)) // MOBA_BLOCK
            ind = jnp.where(key_blk == blk, 1.0, 0.0).astype(BF16)
            ksum = ksum + _dot(ind, kr_ref[0, hd, j * tq:(j + 1) * tq, :])
        qk0 = HEAD_DIM * (hd % 2)
        sel0 = HEAD_DIM - qk0 + SEL_OFF
        is_k = (lane >= qk0) & (lane < qk0 + HEAD_DIM)
        kmean = jnp.where(is_k, ksum * (1.0 / MOBA_BLOCK), 0.0).astype(BF16)
        for qi in range(nq):
            q0 = qt0_ref[0, hd, qi]
            cur = (qi * tq + lax.broadcasted_iota(jnp.int32, (nblk, tq), 1)) // MOBA_BLOCK
            valid = blk < cur
            gate = jnp.where(valid, _dot(kmean, q0), -jnp.inf)
            rank = jnp.zeros((nblk, tq), F32)
            for m in range(nblk):
                gm = gate[m:m + 1, :]
                ahead = (gm > gate) | ((gm == gate) & (blk > m))
                rank = rank + jnp.where(ahead, 1.0, 0.0)
            keep = (valid & (rank < MOBA_TOPK)) | (blk == cur)
            selb = jnp.where(keep, 0.0, NEG).astype(BF16)
            parts = ([q0[:sel0]] if sel0 else []) + [selb, q0[sel0 + nblk:]]
            qt_ref[0, hd, qi] = jnp.concatenate(parts, axis=0)


def _gate_call(qt0, kr):
    b, _, nq, _, tq = qt0.shape
    t = nq * tq
    nblk = t // MOBA_BLOCK
    assert nblk == MOBA_NBLK and SEL_OFF % 16 == 0
    qt_spec = pl.BlockSpec((1, N_GROUP, nq, LANES, tq), lambda bi: (bi, 0, 0, 0, 0))
    kr_spec = pl.BlockSpec((1, N_GROUP, t, LANES), lambda bi: (bi, 0, 0, 0))
    return pl.pallas_call(
        functools.partial(_gate_kernel, t=t, tq=tq),
        out_shape=jax.ShapeDtypeStruct(qt0.shape, BF16),
        grid=(b,),
        in_specs=[qt_spec, kr_spec],
        out_specs=qt_spec,
        compiler_params=pltpu.CompilerParams(dimension_semantics=("arbitrary",),
                                             vmem_limit_bytes=VMEM_LIMIT),
        name="moba_gate",
    )(qt0, kr)


def _attn_kernel(kr_ref, qt_ref, vt_ref, o_ref, acc_ref, *, tq, tk):
    qi = pl.program_id(1)
    ratio = tq // tk
    rowi = lax.broadcasted_iota(jnp.int32, (LANES, tq), 0)
    acc_ref[...] = jnp.zeros(acc_ref.shape, F32)

    def step(j, ms, c0, masked):
        w = tq - c0
        k0 = pl.multiple_of(j * tk, tk)
        sts = [_dot(kr_ref[0, hd, pl.ds(k0, tk), :], qt_ref[0, hd, 0, :, c0:]) for hd in range(N_GROUP)]
        out = []
        for hd in range(N_GROUP):
            st = sts[hd]
            if masked:
                keyi = lax.broadcasted_iota(jnp.int32, (tk, w), 0)
                qryi = lax.broadcasted_iota(jnp.int32, (tk, w), 1)
                st = jnp.where(keyi <= qryi, st, NEG)
            m_old = ms[hd][:, c0:]
            m_new = jnp.maximum(m_old, jnp.max(st, axis=0, keepdims=True))
            p = jnp.exp2(st - m_new)
            pv = _dot(vt_ref[0, hd, j], p.astype(BF16))
            acc_ref[hd, :, c0:] = jnp.exp2(m_old - m_new) * acc_ref[hd, :, c0:] + pv
            out.append(m_new if c0 == 0 else jnp.concatenate([ms[hd][:, :c0], m_new], axis=1))
        return tuple(out)

    def lagged_step(j, carry):
        used, nxt, excess = carry
        k0 = pl.multiple_of(j * tk, tk)
        sts = [_dot(kr_ref[0, hd, pl.ds(k0, tk), :], qt_ref[0, hd, 0]) for hd in range(N_GROUP)]
        new_nxt = []
        for hd in range(N_GROUP):
            st, ref = sts[hd], nxt[hd]
            p = jnp.exp2(st - ref)
            block_max = jnp.max(st, axis=0, keepdims=True)
            pv = _dot(vt_ref[0, hd, j], p.astype(BF16))
            acc_ref[hd] = jnp.exp2(used[hd] - ref) * acc_ref[hd] + pv
            excess = jnp.maximum(excess, block_max - ref)
            new_nxt.append(jnp.maximum(ref, block_max))
        return nxt, tuple(new_nxt), excess

    def diagonal(ms):
        for d in range(ratio):
            ms = step(qi * ratio + d, ms, d * tk, True)
        return ms

    m0 = tuple(jnp.full((1, tq), -jnp.inf, F32) for _ in range(N_GROUP))
    ms = diagonal(m0)
    _, _, excess = lax.fori_loop(0, qi * ratio, lagged_step, (ms, ms, jnp.full((1, tq), -jnp.inf, F32)))

    @pl.when(jnp.max(excess) > LAG_LIMIT)
    def _():
        acc_ref[...] = jnp.zeros(acc_ref.shape, F32)
        diagonal(lax.fori_loop(0, qi * ratio, lambda j, c: step(j, c, 0, False), m0))

    for pair in range(N_GROUP // 2):
        even, odd = acc_ref[2 * pair], acc_ref[2 * pair + 1]
        out_t = jnp.where(rowi < HEAD_DIM, even / even[HEAD_DIM:HEAD_DIM + 1, :], odd / odd[0:1, :])
        o_ref[0, :, pair * LANES:(pair + 1) * LANES] = out_t.T.astype(BF16)


def _attn_call(kr, qt, vt, name):
    b, _, t, _ = kr.shape
    tq, tk = qt.shape[-1], vt.shape[-1]
    return pl.pallas_call(
        functools.partial(_attn_kernel, tq=tq, tk=tk),
        out_shape=jax.ShapeDtypeStruct((b, t, HW), BF16),
        grid=(b, t // tq),
        in_specs=[
            pl.BlockSpec((1, N_GROUP, t, LANES), lambda bi, i: (bi, 0, 0, 0)),
            pl.BlockSpec((1, N_GROUP, 1, LANES, tq), lambda bi, i: (bi, 0, i, 0, 0)),
            pl.BlockSpec((1, N_GROUP, t // tk, LANES, tk), lambda bi, i: (bi, 0, 0, 0, 0)),
        ],
        out_specs=pl.BlockSpec((1, tq, HW), lambda bi, i: (bi, i, 0)),
        scratch_shapes=[pltpu.VMEM((N_GROUP, LANES, tq), F32)],
        compiler_params=pltpu.CompilerParams(dimension_semantics=("arbitrary", "arbitrary"),
                                             vmem_limit_bytes=VMEM_LIMIT),
        name=name,
    )(kr, qt, vt)


DIL_UNROLL = 16


def _dil_kernel(q_ref, k_ref, v_ref, bias_ref, bias2_ref, o_ref, acc_ref, max_ref, *, t):
    span = DIL_SPAN
    c_hd = HEAD_DIM ** -0.5 * LOG2E

    def rows(start, size, dil):
        return pl.ds(start, size) if dil == 1 else pl.ds(start, size, stride=dil)

    def own(shape, hh):
        lane = lax.broadcasted_iota(jnp.int32, shape, 1)
        return (lane < HEAD_DIM) if hh == 0 else (lane >= HEAD_DIM)

    def finish_blocks(work, br, dil):
        for hh, q0, nq, v2, bias, qk in work:
            va = jnp.where(own(v2.shape, hh), v2, 1.0).astype(BF16)
            s = qk + bias
            m = jnp.max(s, axis=1, keepdims=True)
            p = jnp.exp2(s - m)
            acc_ref[br, hh, rows(q0, nq, dil), :] = _dot(p.astype(BF16), va)
            max_ref[br, hh, rows(q0, nq, dil), :] = jnp.broadcast_to(m, (nq, LANES))

    for br, (window, dil) in enumerate(DIL_PAIRS):
        nb = t // dil // span

        def group(it, carry, br=br, dil=dil, nb=nb):
            work = []
            for u in range(DIL_UNROLL):
                idx = it * DIL_UNROLL + u
                r, n = idx // nb, idx % nb
                q0 = r + n * (span * dil)
                first = 1 - jnp.minimum(n, 1)
                k0 = q0 - (1 - first) * (span * dil)
                q2 = q_ref[0, rows(q0, span, dil), :] * c_hd
                k2 = k_ref[0, rows(k0, 2 * span, dil), :].astype(BF16)
                v2 = v_ref[0, rows(k0, 2 * span, dil), :]
                for hh in range(2):
                    qz = jnp.where(own(q2.shape, hh), q2, 0.0).astype(BF16)
                    work.append((hh, q0, span, v2, bias_ref[hh, 2 * br + first], _dot_nt(qz, k2)))
            finish_blocks(work, br, dil)
            return carry

        def group_whole(it, carry, br=br, dil=dil):
            work = []
            for u in range(DIL_UNROLL // 2):
                r = it * (DIL_UNROLL // 2) + u
                q2 = q_ref[0, rows(r, 2 * span, dil), :] * c_hd
                k2 = k_ref[0, rows(r, 2 * span, dil), :].astype(BF16)
                v2 = v_ref[0, rows(r, 2 * span, dil), :]
                for hh in range(2):
                    qz = jnp.where(own(q2.shape, hh), q2, 0.0).astype(BF16)
                    work.append((hh, r, 2 * span, v2, bias2_ref[hh], _dot_nt(qz, k2)))
            finish_blocks(work, br, dil)
            return carry

        if nb == 2:
            lax.fori_loop(0, dil // (DIL_UNROLL // 2), group_whole, 0)
        else:
            lax.fori_loop(0, dil * nb // DIL_UNROLL, group, 0)

    chunk = 256
    lane_c = lax.broadcasted_iota(jnp.int32, (chunk, LANES), 1)

    def finish(ci, carry):
        sl = pl.ds(ci * chunk, chunk)
        outs = []
        for hh in range(2):
            ms = [max_ref[br, hh, sl, :] for br in range(len(DIL_PAIRS))]
            mx = functools.reduce(jnp.maximum, ms)
            tot = sum(acc_ref[br, hh, sl, :] * jnp.exp2(m - mx) for br, m in enumerate(ms))
            outs.append(tot / pltpu.roll(tot, HEAD_DIM, axis=1))
        o_ref[0, sl, :] = jnp.where(lane_c < HEAD_DIM, outs[0], outs[1]).astype(BF16)
        return carry

    lax.fori_loop(0, t // chunk, finish, 0)


def _dil_bias_table():
    qi = np.arange(DIL_SPAN)[:, None]
    kj = np.arange(2 * DIL_SPAN)[None, :]
    tab = np.zeros((N_GROUP, 2 * len(DIL_PAIRS), DIL_SPAN, 2 * DIL_SPAN), np.float32)
    for variant in range(2):
        delta = qi - kj + (DIL_SPAN if variant == 0 else 0)
        valid = (delta >= 0) & (delta <= DIL_SPAN)
        for hd in range(N_GROUP):
            for br, (_, dil) in enumerate(DIL_PAIRS):
                tab[hd, 2 * br + variant] = np.where(valid, -SLOPES_DIL[hd] * LOG2E * (delta * dil), NEG)
    return tab


def _dil_bias_whole(dil):
    delta = np.arange(2 * DIL_SPAN)[:, None] - np.arange(2 * DIL_SPAN)[None, :]
    valid = (delta >= 0) & (delta <= DIL_SPAN)
    return np.stack([np.where(valid, -SLOPES_DIL[hd] * LOG2E * (delta * dil), NEG)
                     for hd in range(N_GROUP)]).astype(np.float32)


def _dil_call(yd, bias, bias2):
    b, t, _ = yd.shape
    for window, dil in DIL_PAIRS:
        nb = t // dil // DIL_SPAN
        assert window // dil == DIL_SPAN and nb * dil * DIL_SPAN == t and nb >= 2
        assert (dil * nb) % DIL_UNROLL == 0
    n_pair = N_GROUP // 2
    n_br = len(DIL_PAIRS)
    return pl.pallas_call(
        functools.partial(_dil_kernel, t=t),
        out_shape=jax.ShapeDtypeStruct((b, t, HW), BF16),
        grid=(b, n_pair),
        in_specs=[
            pl.BlockSpec((1, t, LANES), lambda bi, p: (bi, 0, p)),
            pl.BlockSpec((1, t, LANES), lambda bi, p: (bi, 0, n_pair + p)),
            pl.BlockSpec((1, t, LANES), lambda bi, p: (bi, 0, 2 * n_pair + p)),
            pl.BlockSpec((2, 2 * n_br, DIL_SPAN, 2 * DIL_SPAN), lambda bi, p: (p, 0, 0, 0)),
            pl.BlockSpec((2, 2 * DIL_SPAN, 2 * DIL_SPAN), lambda bi, p: (p, 0, 0)),
        ],
        out_specs=pl.BlockSpec((1, t, LANES), lambda bi, p: (bi, 0, p)),
        scratch_shapes=[pltpu.VMEM((n_br, 2, t, LANES), F32), pltpu.VMEM((n_br, 2, t, LANES), F32)],
        compiler_params=pltpu.CompilerParams(dimension_semantics=("arbitrary", "arbitrary"),
                                             vmem_limit_bytes=VMEM_LIMIT),
        name="dilated",
    )(yd, yd, yd, bias, bias2)


def _cols(w, spec):
    parts = []
    for item in spec:
        if isinstance(item, int):
            parts.append(jnp.zeros((w.shape[0], item), w.dtype))
        else:
            piece = w[:, item[0]:item[0] + item[1]]
            parts.append(-piece if len(item) == 3 else piece)
    return jnp.concatenate(parts, axis=1)


def _layout_specs():
    half = MLA_ROPE // 2
    pad = LANES - HEAD_DIM
    aligned = [(0, CQ0), PAD_F, (CQ0, DQ0 - CQ0), PAD_R, (DQ0, IN_WIDTH - DQ0)]
    extra = np.zeros((N_GROUP + MLA_ROPE, WX_END), np.float32)
    for base in (0, HEAD_DIM):
        for grp in range(2):
            for hd in range(N_GROUP):
                c0 = WX_F + base + (grp * N_GROUP + hd) * N_PIECE
                extra[hd, c0:c0 + N_PIECE] = 1.0
    for r in range(MLA_ROPE):
        extra[N_GROUP + r, WX_KR + HEAD_DIM + r] = 1.0
        dst = r - half if r >= half else r + half
        extra[N_GROUP + r, WX_KRH + HEAD_DIM + dst] = -1.0 if r >= half else 1.0
    qw = HEAD_DIM + MLA_ROPE
    uq = [s for hd in range(N_GROUP) for s in ((hd * qw, qw), LANES - qw)]
    uqrh = [s for hd in range(N_GROUP)
            for s in (HEAD_DIM, (hd * qw + HEAD_DIM + half, half, -1.0), (hd * qw + HEAD_DIM, half), LANES - qw)]
    kn = [s for hd in range(N_GROUP) for s in ((hd * LANES, HEAD_DIM), pad)]
    vv = [(hd * LANES + HEAD_DIM, HEAD_DIM) for hd in range(N_GROUP)]
    q_in, kv_in = N_GROUP * qw, N_GROUP * LANES
    return (aligned, extra, _selection(q_in, uq), _selection(q_in, uqrh), _selection(kv_in, kn),
            _selection(kv_in, vv))


def _selection(n_in, spec):
    width = sum(item if isinstance(item, int) else item[1] for item in spec)
    sel = np.zeros((n_in, width), np.float32)
    col = 0
    for item in spec:
        if isinstance(item, int):
            col += item
            continue
        for j in range(item[1]):
            sel[item[0] + j, col + j] = -1.0 if len(item) == 3 else 1.0
        col += item[1]
    return sel


def _relayout(w, sel):
    return jnp.dot(w, jnp.asarray(sel, BF16), preferred_element_type=F32).astype(BF16)


_SPECS = _layout_specs()


def _rope_tables(t):
    inv = ROPE_THETA ** (-jnp.arange(0, MLA_ROPE, 2, dtype=F32) / MLA_ROPE)
    ang = jnp.arange(t, dtype=F32)[:, None] * inv[None, :]
    cos, sin = jnp.cos(ang), jnp.sin(ang)
    z_lo = jnp.zeros((t, HEAD_DIM), F32)
    z_hi = jnp.zeros((t, LANES - HEAD_DIM - MLA_ROPE), F32)
    cosq = jnp.concatenate([jnp.ones((t, HEAD_DIM), F32), cos, cos, z_hi], axis=1)
    sin_tab = jnp.concatenate([z_lo, sin, sin, z_hi], axis=1)
    cosk = jnp.concatenate([z_lo, cos, cos, z_hi], axis=1)
    return cosq.T, sin_tab.T, cosk, sin_tab


def kernel(x, norm_ffn1, ffn1_gate, ffn1_up, ffn1_down, norm_mix, w_in, forget_bias, mla_q_norm,
           mla_kv_norm, w_uq, w_ukv, w_out, norm_ffn2, ffn2_gate, ffn2_up, ffn2_down, norm_final):
    b, t, d = x.shape
    depth = w_in.shape[0]
    assert d == D_MODEL and t % ATT_TQ == 0 and ATT_TQ % TILE == 0 and (b * t) % FFN_TILE == 0
    aligned_spec, extra_spec, uq_spec, uqrh_spec, kn_spec, vv_spec = _SPECS
    cosq, sinq, cosk, sink = _rope_tables(t)
    tri = jnp.tril(jnp.ones((TILE, TILE), F32)).astype(BF16)
    dil_bias = jnp.asarray(_dil_bias_table())
    whole = [dil for _, dil in DIL_PAIRS if t // dil // DIL_SPAN == 2]
    dil_bias2 = jnp.asarray(_dil_bias_whole(whole[0] if whole else 1))
    lane = np.arange(LANES)
    fg_off = lane % HEAD_DIM
    in_f = fg_off < FG_W
    fg_head = np.where(in_f, (fg_off // N_PIECE) % N_GROUP, 0)
    fg_piece = np.where(in_f, fg_off % N_PIECE, -1).astype(np.int32)
    pq, pk = jnp.asarray(fg_piece[None, :]), jnp.asarray(fg_piece[:, None])
    bf = lambda w: w.astype(BF16)
    row = lambda v: v.reshape(1, -1)

    xf = x.reshape(b * t, d)
    for l in range(depth):
        xf = _ffn_call(xf, None, None, row(norm_ffn1[l]), ffn1_gate, ffn1_up, ffn1_down, l, None)
        w_l = bf(w_in[l])
        wal = _cols(w_l, aligned_spec)
        w_fr = jnp.concatenate([w_l[:, BF0:BF0 + N_GROUP], w_l[:, CKR0:CKR0 + MLA_ROPE]], axis=1)
        wx = _relayout(w_fr, extra_spec)
        wnt = w_l[:, :NT_END].T
        fb = jnp.where(jnp.asarray(in_f), jnp.take(forget_bias[l], jnp.asarray(fg_head)), 0.0)[None, :]
        wuq_l, wukv_l = bf(w_uq[l]), bf(w_ukv[l])
        wuqt = _relayout(wuq_l, uq_spec).T
        wuqrht = _relayout(wuq_l, uqrh_spec).T
        wkn = _relayout(wukv_l, kn_spec)
        wvt = _relayout(wukv_l, vv_spec).T
        (kr_a, qt_a0, vt_a, kr_b, qt_b, vt_b, kr_c, qt_c, vt_c, yd) = _proj_call(
            xf.reshape(b, t, d), row(norm_mix[l]), wal, wx, wnt, fb, pq, pk, tri,
            row(mla_q_norm[l]), row(mla_kv_norm[l]), wuqt, wuqrht, wkn, wvt, cosq, sinq, cosk, sink)
        qt_a = _gate_call(qt_a0, kr_a)
        o_a = _attn_call(kr_a, qt_a, vt_a, "attn_moba")
        o_b = _attn_call(kr_b, qt_b, vt_b, "attn_forget")
        o_c = _attn_call(kr_c, qt_c, vt_c, "attn_mla")
        o_d = _dil_call(yd, dil_bias, dil_bias2)
        mix = [o.reshape(b * t, HW) for o in (o_a, o_b, o_c, o_d)]
        xf = _ffn_call(xf, mix, bf(w_out[l]), row(norm_ffn2[l]), ffn2_gate, ffn2_up, ffn2_down, l,
                       row(norm_final) if l == depth - 1 else None)
    return xf.reshape(b, t, d)
```

```python
import functools

import numpy as np
import jax
import jax.numpy as jnp
from jax import lax
from jax.experimental import pallas as pl
from jax.experimental.pallas import tpu as pltpu

F32 = jnp.float32
BF16 = jnp.bfloat16

D_MODEL = 1024
HEAD_DIM = 64
N_GROUP = 4
HW = N_GROUP * HEAD_DIM
LANES = 128
MOBA_BLOCK = 256
MOBA_TOPK = 3
MLA_Q_RANK = 256
MLA_KV_RANK = 128
MLA_ROPE = 32
ROPE_THETA = 10000.0
DIL_PAIRS = ((128, 1), (512, 4), (2048, 16))
DIL_SPAN = 128
D_FF = 2816
RMS_EPS = 1e-6
NEG = -1e30
LOG2E = 1.4426950408889634

N_PIECE = 3
MOBA_NBLK = 16
SEL_OFF = 0
ONE_OFF = 16
PIECE_OFF = 19
FGA_OFF = 0
FGB_OFF = N_GROUP * N_PIECE
FG_W = 2 * N_GROUP * N_PIECE

TILE = 512
ATT_TQ = 1024
LAG_LIMIT = 64.0
FFN_TILE = 512
FF_CHUNK = 256
VMEM_LIMIT = 56 * 1024 * 1024

_ALIBI = [2.0 ** (-8.0 * (i + 1) / 8) for i in range(8)]
SLOPES_MOBA = _ALIBI[1::2]
SLOPES_DIL = _ALIBI[0::2]

_WIDTHS = (HW, HW, HW, HW, HW, HW, N_GROUP, MLA_Q_RANK, MLA_KV_RANK, MLA_ROPE, HW, HW, HW)
_OFF = np.concatenate([[0], np.cumsum(_WIDTHS)]).astype(np.int64)
(AQ0, AK0, AV0, BQ0, BK0, BV0, BF0, CQ0, CKV0, CKR0, DQ0, DK0, DV0, IN_WIDTH) = [int(v) for v in _OFF]


def _dot(a, b):
    return jnp.dot(a, b, preferred_element_type=F32)


def _dot_nt(a, b):
    return lax.dot_general(a, b, (((1,), (1,)), ((), ())), preferred_element_type=F32)


def _rms(x, g):
    return x * lax.rsqrt(jnp.mean(x * x, axis=-1, keepdims=True) + RMS_EPS) * g


def _split3(x):
    p0 = x.astype(BF16)
    r = x - p0.astype(F32)
    p1 = r.astype(BF16)
    p2 = (r - p1.astype(F32)).astype(BF16)
    return p0, p1, p2


def _log_sigmoid(x):
    return jnp.minimum(x, 0.0) - jnp.log(1.0 + jnp.exp(-jnp.abs(x)))


def _ffn_kernel(*refs, has_mix, final_norm):
    it = iter(refs)
    x_ref = next(it)
    if has_mix:
        o_refs = [next(it) for _ in range(4)]
        wo_ref = next(it)
    g_ref, wg_ref, wu_ref, wd_ref = next(it), next(it), next(it), next(it)
    gf_ref = next(it) if final_norm else None
    out_ref = next(it)

    x = x_ref[...]
    if has_mix:
        for i, o_ref in enumerate(o_refs):
            x = x + _dot(o_ref[...], wo_ref[i * HW:(i + 1) * HW, :])
    h = _rms(x, g_ref[...]).astype(BF16)
    y = jnp.zeros(x.shape, F32)
    for c in range(D_FF // FF_CHUNK):
        sl = slice(c * FF_CHUNK, (c + 1) * FF_CHUNK)
        gate = _dot(h, wg_ref[:, sl].astype(BF16))
        up = _dot(h, wu_ref[:, sl].astype(BF16))
        act = (gate * jax.nn.sigmoid(gate) * up).astype(BF16)
        y = y + _dot(act, wd_ref[sl, :].astype(BF16))
    x = x + 0.5 * y
    if final_norm:
        x = _rms(x, gf_ref[...])
    out_ref[...] = x


def _const_spec(shape):
    nd = len(shape)
    return pl.BlockSpec(shape, lambda *_: (0,) * nd, pipeline_mode=pl.Buffered(1))


def _layer_spec(shape, layer):
    nd = len(shape)
    return pl.BlockSpec((None,) + tuple(shape), lambda *_: (layer,) + (0,) * nd, pipeline_mode=pl.Buffered(1))


def _ffn_call(x, mix, w_out, g, wg, wu, wd, layer, g_final):
    n = x.shape[0]
    tm = FFN_TILE
    has_mix = mix is not None
    final_norm = g_final is not None
    row = lambda i: (i, 0)
    args, specs = [x], [pl.BlockSpec((tm, D_MODEL), row)]
    if has_mix:
        for o in mix:
            args.append(o)
            specs.append(pl.BlockSpec((tm, HW), row))
        args.append(w_out)
        specs.append(_const_spec((D_MODEL, D_MODEL)))
    args += [g, wg, wu, wd]
    specs += [_const_spec((1, D_MODEL)), _layer_spec((D_MODEL, D_FF), layer), _layer_spec((D_MODEL, D_FF), layer),
              _layer_spec((D_FF, D_MODEL), layer)]
    if final_norm:
        args.append(g_final)
        specs.append(_const_spec((1, D_MODEL)))
    return pl.pallas_call(
        functools.partial(_ffn_kernel, has_mix=has_mix, final_norm=final_norm),
        out_shape=jax.ShapeDtypeStruct((n, D_MODEL), F32),
        grid=(n // tm,),
        in_specs=specs,
        out_specs=pl.BlockSpec((tm, D_MODEL), row),
        compiler_params=pltpu.CompilerParams(dimension_semantics=("arbitrary",),
                                             vmem_limit_bytes=VMEM_LIMIT),
        name="ffn_mix" if has_mix else "ffn",
    )(*args)


PAD_F = -(BF0 + N_GROUP) % LANES
PAD_R = -(CKR0 + MLA_ROPE + PAD_F) % LANES
AL_AK, AL_BK, AL_CQ = AK0, BK0, CQ0 + PAD_F
AL_CKV = CKV0 + PAD_F
AL_D = DQ0 + PAD_F + PAD_R
AL_END = IN_WIDTH + PAD_F + PAD_R
assert all(v % LANES == 0 for v in (AL_AK, AL_BK, AL_CQ, AL_CKV, AL_D, AL_END)) and AL_D - AL_CKV == 2 * LANES
WX_F, WX_KR, WX_KRH, WX_END = 0, LANES, 2 * LANES, 3 * LANES
NT_AQ, NT_AV, NT_BQ, NT_BV, NT_END = AQ0, AV0, BQ0, BV0, BF0


def _pieces_select(idx, base, val):
    p0, p1, p2 = _split3(val)
    return jnp.where(idx == base, p0.astype(F32),
                     jnp.where(idx == base + 1, p1.astype(F32), p2.astype(F32)))


def _pieces_by(which, val):
    p0, p1, p2 = _split3(val)
    return jnp.where(which == 0, p0.astype(F32), jnp.where(which == 1, p1.astype(F32), p2.astype(F32)))


def _vt_with_ones(vt, head):
    row = lax.broadcasted_iota(jnp.int32, vt.shape, 0)
    keep = (row < HEAD_DIM) if head % 2 == 0 else (row >= HEAD_DIM)
    return jnp.where(keep, vt, 1.0).astype(BF16)


def _proj_kernel(x_ref, g_ref, w_ref, wx_ref, fb_ref, pq_ref, pk_ref, tri_ref,
                 qn_ref, kvn_ref, wuqt_ref, wuqrht_ref, wkn_ref, wvt_ref,
                 cosq_ref, sinq_ref, cosk_ref, sink_ref,
                 kr_a, qt_a, vt_a, kr_b, qt_b, vt_b, kr_c, qt_c, vt_c, yd_ref,
                 carry, wal_ref, wnt_ref, *, tm):
    i = pl.program_id(1)

    @pl.when((pl.program_id(0) == 0) & (i == 0))
    def _():
        for src, dst in ((AK0, AL_AK), (BK0, AL_BK)):
            wal_ref[:, dst:dst + HW] = w_ref[:, src:src + HW].astype(BF16)
        wal_ref[:, AL_CQ:AL_CQ + DQ0 - CQ0] = w_ref[:, CQ0:DQ0].astype(BF16)
        wal_ref[:, AL_D - PAD_R:AL_D] = jnp.zeros((D_MODEL, PAD_R), BF16)
        wal_ref[:, AL_D:AL_END] = w_ref[:, DQ0:IN_WIDTH].astype(BF16)
        for src in (NT_AQ, NT_AV, NT_BQ, NT_BV):
            wnt_ref[src:src + HW, :] = w_ref[:, src:src + HW].T.astype(BF16)

    h = _rms(x_ref[0], g_ref[...]).astype(BF16)
    lane = lax.broadcasted_iota(jnp.int32, (tm, LANES), 1)
    rowi = lax.broadcasted_iota(jnp.int32, (LANES, tm), 0)
    tok_r_i = i * tm + lax.broadcasted_iota(jnp.int32, (tm, LANES), 0)
    tok_r = tok_r_i.astype(F32)
    tok_c = (i * tm + lax.broadcasted_iota(jnp.int32, (LANES, tm), 1)).astype(F32)
    c_hd = HEAD_DIM ** -0.5 * LOG2E
    head_sl = lambda hd: slice(hd * LANES, (hd + 1) * LANES)

    @pl.when(i == 0)
    def _():
        carry[...] = jnp.zeros_like(carry)

    def feature_masks(idx, hd):
        qk0 = HEAD_DIM * (hd % 2)
        return (idx >= qk0) & (idx < qk0 + HEAD_DIM), idx - (HEAD_DIM - qk0)

    between = lambda off, lo, n: (off >= lo) & (off < lo + n)

    zk = _dot(h, wal_ref[:, AL_AK:AL_AK + HW])
    zq = _dot_nt(wnt_ref[NT_AQ:NT_AQ + HW, :], h)
    zv = _dot_nt(wnt_ref[NT_AV:NT_AV + HW, :], h)
    for hd in range(N_GROUP):
        slope2 = SLOPES_MOBA[hd] * LOG2E
        pair = head_sl(hd // 2)
        is_k, off = feature_masks(lane, hd)
        ind = between(off, SEL_OFF, MOBA_NBLK) & ((tok_r_i // MOBA_BLOCK) == off - SEL_OFF)
        ext = _pieces_select(off, ONE_OFF, tok_r * slope2)
        kr_a[0, hd] = jnp.where(is_k, zk[:, pair],
                                jnp.where(between(off, ONE_OFF, N_PIECE), ext,
                                          jnp.where(between(off, PIECE_OFF, N_PIECE) | ind, 1.0, 0.0))).astype(BF16)
        is_q, off = feature_masks(rowi, hd)
        ext = _pieces_select(off, PIECE_OFF, tok_c * (-slope2))
        qt_a[0, hd, 0] = jnp.where(is_q, zq[pair, :] * c_hd,
                                   jnp.where(between(off, ONE_OFF, N_PIECE), 1.0,
                                             jnp.where(between(off, PIECE_OFF, N_PIECE), ext, 0.0))).astype(BF16)
        vt_a[0, hd, 0] = _vt_with_ones(zv[pair, :], hd)

    zfr = _dot(h, wx_ref[:, WX_F:WX_KRH])
    in_f = (lane & (HEAD_DIM - 1)) < FG_W
    lf = jnp.where(in_f, _log_sigmoid(zfr[:, :LANES] + fb_ref[...]), 0.0)
    p0, p1, p2 = _split3(lf)
    tri = tri_ref[...]
    dec = _dot(tri, p0) + _dot(tri, p1) + _dot(tri, p2) + carry[0:1, :]
    carry[0:1, :] = dec[tm - 1:tm, :]
    e_k = _pieces_by(pq_ref[...], dec * (-LOG2E))
    e_q = _pieces_by(pk_ref[...], dec.T * LOG2E)
    zk = _dot(h, wal_ref[:, AL_BK:AL_BK + HW])
    zq = _dot_nt(wnt_ref[NT_BQ:NT_BQ + HW, :], h)
    zv = _dot_nt(wnt_ref[NT_BV:NT_BV + HW, :], h)
    for hd in range(N_GROUP):
        pair = head_sl(hd // 2)
        is_k, off = feature_masks(lane, hd)
        kr_b[0, hd] = jnp.where(is_k, zk[:, pair],
                                jnp.where(between(off, FGA_OFF, FG_W // 2), e_k,
                                          jnp.where(between(off, FGB_OFF + N_PIECE * hd, N_PIECE), 1.0, 0.0))
                                ).astype(BF16)
        is_q, off = feature_masks(rowi, hd)
        qt_b[0, hd, 0] = jnp.where(is_q, zq[pair, :] * c_hd,
                                   jnp.where(between(off, FGB_OFF, FG_W // 2), e_q,
                                             jnp.where(between(off, FGA_OFF + N_PIECE * hd, N_PIECE), 1.0, 0.0))
                                   ).astype(BF16)
        vt_b[0, hd, 0] = _vt_with_ones(zv[pair, :], hd)

    c_mla = (HEAD_DIM + MLA_ROPE) ** -0.5 * LOG2E
    cqn = _rms(_dot(h, wal_ref[:, AL_CQ:AL_CQ + MLA_Q_RANK]), qn_ref[...]).astype(BF16)
    zc = _dot(h, wal_ref[:, AL_CKV:AL_D])
    ckvn = _rms(zc[:, :MLA_KV_RANK], kvn_ref[...]).astype(BF16)
    zrh = _dot(h, wx_ref[:, WX_KRH:WX_END])
    rot_k = zfr[:, LANES:] * cosk_ref[...] + zrh * sink_ref[...]
    kn = _dot(ckvn, wkn_ref[...])
    qt = _dot_nt(wuqt_ref[...], cqn)
    qrht = _dot_nt(wuqrht_ref[...], cqn)
    vt = _dot_nt(wvt_ref[...], ckvn)
    cosq, sinq = cosq_ref[...], sinq_ref[...]
    for hd in range(N_GROUP):
        kr_c[0, hd] = (kn[:, head_sl(hd)] + rot_k).astype(BF16)
        qt_c[0, hd, 0] = ((qt[head_sl(hd), :] * cosq + qrht[head_sl(hd), :] * sinq) * c_mla).astype(BF16)
        vt_c[0, hd, 0] = _vt_with_ones(vt[head_sl(hd // 2), :], hd)

    yd_ref[0] = _dot(h, wal_ref[:, AL_D:AL_END])


def _proj_call(x3, g, w_in, layer, wx, fb, pq, pk, tri, qn, kvn, wuqt, wuqrht, wkn, wvt, cosq, sinq, cosk, sink):
    b, t, _ = x3.shape
    tm = TILE
    nk = t // tm
    per_q = ATT_TQ // tm
    kr_shape = jax.ShapeDtypeStruct((b, N_GROUP, t, LANES), BF16)
    vt_shape = jax.ShapeDtypeStruct((b, N_GROUP, nk, LANES, tm), BF16)
    qt_shape = jax.ShapeDtypeStruct((b, N_GROUP, t // ATT_TQ, LANES, ATT_TQ), BF16)
    kr_spec = pl.BlockSpec((1, N_GROUP, tm, LANES), lambda bi, i: (bi, 0, i, 0))
    vt_spec = pl.BlockSpec((1, N_GROUP, 1, LANES, tm), lambda bi, i: (bi, 0, i, 0, 0))
    qt_spec = pl.BlockSpec((1, N_GROUP, 1, LANES, tm), lambda bi, i: (bi, 0, i // per_q, 0, i % per_q))
    in_specs = [
        pl.BlockSpec((1, tm, D_MODEL), lambda bi, i: (bi, i, 0)),
        _const_spec((1, D_MODEL)),
        _layer_spec((D_MODEL, IN_WIDTH), layer),
        _const_spec((D_MODEL, WX_END)),
        _const_spec((1, LANES)),
        _const_spec((1, LANES)),
        _const_spec((LANES, 1)),
        _const_spec((tm, tm)),
        _const_spec((1, MLA_Q_RANK)),
        _const_spec((1, MLA_KV_RANK)),
        _const_spec((N_GROUP * LANES, MLA_Q_RANK)),
        _const_spec((N_GROUP * LANES, MLA_Q_RANK)),
        _const_spec((MLA_KV_RANK, N_GROUP * LANES)),
        _const_spec((HW, MLA_KV_RANK)),
        pl.BlockSpec((LANES, tm), lambda bi, i: (0, i)),
        pl.BlockSpec((LANES, tm), lambda bi, i: (0, i)),
        pl.BlockSpec((tm, LANES), lambda bi, i: (i, 0)),
        pl.BlockSpec((tm, LANES), lambda bi, i: (i, 0)),
    ]
    out_shape = [kr_shape, qt_shape, vt_shape] * 3 + [jax.ShapeDtypeStruct((b, t, 3 * HW), F32)]
    out_specs = [kr_spec, qt_spec, vt_spec] * 3 + [pl.BlockSpec((1, tm, 3 * HW), lambda bi, i: (bi, i, 0))]
    return pl.pallas_call(
        functools.partial(_proj_kernel, tm=tm),
        out_shape=out_shape,
        grid=(b, nk),
        in_specs=in_specs,
        out_specs=out_specs,
        scratch_shapes=[pltpu.VMEM((8, LANES), F32), pltpu.VMEM((D_MODEL, AL_END), BF16),
                        pltpu.VMEM((NT_END, D_MODEL), BF16)],
        compiler_params=pltpu.CompilerParams(dimension_semantics=("arbitrary", "arbitrary"),
                                             vmem_limit_bytes=VMEM_LIMIT),
        name="proj",
    )(x3, g, w_in, wx, fb, pq, pk, tri, qn, kvn, wuqt, wuqrht, wkn, wvt, cosq, sinq, cosk, sink)


def _gate_kernel(qt0_ref, kr_ref, qt_ref, *, t, tq):
    nq = t // tq
    nblk = t // MOBA_BLOCK
    blk = lax.broadcasted_iota(jnp.int32, (nblk, tq), 0)
    lane = lax.broadcasted_iota(jnp.int32, (nblk, LANES), 1)
    for hd in range(N_GROUP):
        ksum = jnp.zeros((nblk, LANES), F32)
        for j in range(nq):
            key_blk = (j * tq + lax.broadcasted_iota(jnp.int32, (nblk, tq), 1)) // MOBA_BLOCK
            ind = jnp.where(key_blk == blk, 1.0, 0.0).astype(BF16)
            ksum = ksum + _dot(ind, kr_ref[0, hd, j * tq:(j + 1) * tq, :])
        qk0 = HEAD_DIM * (hd % 2)
        sel0 = HEAD_DIM - qk0 + SEL_OFF
        is_k = (lane >= qk0) & (lane < qk0 + HEAD_DIM)
        kmean = jnp.where(is_k, ksum * (1.0 / MOBA_BLOCK), 0.0).astype(BF16)
        for qi in range(nq):
            q0 = qt0_ref[0, hd, qi]
            cur = (qi * tq + lax.broadcasted_iota(jnp.int32, (nblk, tq), 1)) // MOBA_BLOCK
            valid = blk < cur
            gate = jnp.where(valid, _dot(kmean, q0), -jnp.inf)
            rank = jnp.zeros((nblk, tq), F32)
            for m in range(nblk):
                gm = gate[m:m + 1, :]
                ahead = (gm > gate) | ((gm == gate) & (blk > m))
                rank = rank + jnp.where(ahead, 1.0, 0.0)
            keep = (valid & (rank < MOBA_TOPK)) | (blk == cur)
            selb = jnp.where(keep, 0.0, NEG).astype(BF16)
            parts = ([q0[:sel0]] if sel0 else []) + [selb, q0[sel0 + nblk:]]
            qt_ref[0, hd, qi] = jnp.concatenate(parts, axis=0)


def _gate_call(qt0, kr):
    b, _, nq, _, tq = qt0.shape
    t = nq * tq
    nblk = t // MOBA_BLOCK
    assert nblk == MOBA_NBLK and SEL_OFF % 16 == 0
    qt_spec = pl.BlockSpec((1, N_GROUP, nq, LANES, tq), lambda bi: (bi, 0, 0, 0, 0))
    kr_spec = pl.BlockSpec((1, N_GROUP, t, LANES), lambda bi: (bi, 0, 0, 0))
    return pl.pallas_call(
        functools.partial(_gate_kernel, t=t, tq=tq),
        out_shape=jax.ShapeDtypeStruct(qt0.shape, BF16),
        grid=(b,),
        in_specs=[qt_spec, kr_spec],
        out_specs=qt_spec,
        compiler_params=pltpu.CompilerParams(dimension_semantics=("arbitrary",),
                                             vmem_limit_bytes=VMEM_LIMIT),
        name="moba_gate",
    )(qt0, kr)


def _attn_kernel(kr_ref, qt_ref, vt_ref, o_ref, acc_ref, *, tq, tk):
    qi = pl.program_id(1)
    ratio = tq // tk
    rowi = lax.broadcasted_iota(jnp.int32, (LANES, tq), 0)
    acc_ref[...] = jnp.zeros(acc_ref.shape, F32)

    def step(j, ms, c0, masked):
        w = tq - c0
        k0 = pl.multiple_of(j * tk, tk)
        sts = [_dot(kr_ref[0, hd, pl.ds(k0, tk), :], qt_ref[0, hd, 0, :, c0:]) for hd in range(N_GROUP)]
        out = []
        for hd in range(N_GROUP):
            st = sts[hd]
            if masked:
                keyi = lax.broadcasted_iota(jnp.int32, (tk, w), 0)
                qryi = lax.broadcasted_iota(jnp.int32, (tk, w), 1)
                st = jnp.where(keyi <= qryi, st, NEG)
            m_old = ms[hd][:, c0:]
            m_new = jnp.maximum(m_old, jnp.max(st, axis=0, keepdims=True))
            p = jnp.exp2(st - m_new)
            pv = _dot(vt_ref[0, hd, j], p.astype(BF16))
            acc_ref[hd, :, c0:] = jnp.exp2(m_old - m_new) * acc_ref[hd, :, c0:] + pv
            out.append(m_new if c0 == 0 else jnp.concatenate([ms[hd][:, :c0], m_new], axis=1))
        return tuple(out)

    def lagged_step(j, carry):
        used, nxt, excess = carry
        k0 = pl.multiple_of(j * tk, tk)
        sts = [_dot(kr_ref[0, hd, pl.ds(k0, tk), :], qt_ref[0, hd, 0]) for hd in range(N_GROUP)]
        new_nxt = []
        for hd in range(N_GROUP):
            st, ref = sts[hd], nxt[hd]
            p = jnp.exp2(st - ref)
            block_max = jnp.max(st, axis=0, keepdims=True)
            pv = _dot(vt_ref[0, hd, j], p.astype(BF16))
            acc_ref[hd] = jnp.exp2(used[hd] - ref) * acc_ref[hd] + pv
            excess = jnp.maximum(excess, block_max - ref)
            new_nxt.append(jnp.maximum(ref, block_max))
        return nxt, tuple(new_nxt), excess

    def diagonal(ms):
        for d in range(ratio):
            ms = step(qi * ratio + d, ms, d * tk, True)
        return ms

    m0 = tuple(jnp.full((1, tq), -jnp.inf, F32) for _ in range(N_GROUP))
    ms = diagonal(m0)
    _, _, excess = lax.fori_loop(0, qi * ratio, lagged_step, (ms, ms, jnp.full((1, tq), -jnp.inf, F32)))

    @pl.when(jnp.max(excess) > LAG_LIMIT)
    def _():
        acc_ref[...] = jnp.zeros(acc_ref.shape, F32)
        diagonal(lax.fori_loop(0, qi * ratio, lambda j, c: step(j, c, 0, False), m0))

    for pair in range(N_GROUP // 2):
        even, odd = acc_ref[2 * pair], acc_ref[2 * pair + 1]
        out_t = jnp.where(rowi < HEAD_DIM, even / even[HEAD_DIM:HEAD_DIM + 1, :], odd / odd[0:1, :])
        o_ref[0, :, pair * LANES:(pair + 1) * LANES] = out_t.T.astype(BF16)


def _attn_call(kr, qt, vt, name):
    b, _, t, _ = kr.shape
    tq, tk = qt.shape[-1], vt.shape[-1]
    return pl.pallas_call(
        functools.partial(_attn_kernel, tq=tq, tk=tk),
        out_shape=jax.ShapeDtypeStruct((b, t, HW), BF16),
        grid=(b, t // tq),
        in_specs=[
            pl.BlockSpec((1, N_GROUP, t, LANES), lambda bi, i: (bi, 0, 0, 0)),
            pl.BlockSpec((1, N_GROUP, 1, LANES, tq), lambda bi, i: (bi, 0, i, 0, 0)),
            pl.BlockSpec((1, N_GROUP, t // tk, LANES, tk), lambda bi, i: (bi, 0, 0, 0, 0)),
        ],
        out_specs=pl.BlockSpec((1, tq, HW), lambda bi, i: (bi, i, 0)),
        scratch_shapes=[pltpu.VMEM((N_GROUP, LANES, tq), F32)],
        compiler_params=pltpu.CompilerParams(dimension_semantics=("arbitrary", "arbitrary"),
                                             vmem_limit_bytes=VMEM_LIMIT),
        name=name,
    )(kr, qt, vt)


DIL_UNROLL = 16


def _dil_kernel(q_ref, k_ref, v_ref, bias_ref, bias2_ref, o_ref, acc_ref, max_ref, *, t):
    span = DIL_SPAN
    c_hd = HEAD_DIM ** -0.5 * LOG2E

    def rows(start, size, dil):
        return pl.ds(start, size) if dil == 1 else pl.ds(start, size, stride=dil)

    def own(shape, hh):
        lane = lax.broadcasted_iota(jnp.int32, shape, 1)
        return (lane < HEAD_DIM) if hh == 0 else (lane >= HEAD_DIM)

    def finish_blocks(work, br, dil):
        for hh, q0, nq, v2, bias, qk in work:
            va = jnp.where(own(v2.shape, hh), v2, 1.0).astype(BF16)
            s = qk + bias
            m = jnp.max(s, axis=1, keepdims=True)
            p = jnp.exp2(s - m)
            acc_ref[br, hh, rows(q0, nq, dil), :] = _dot(p.astype(BF16), va)
            max_ref[br, hh, rows(q0, nq, dil), :] = jnp.broadcast_to(m, (nq, LANES))

    for br, (window, dil) in enumerate(DIL_PAIRS):
        nb = t // dil // span

        def group(it, carry, br=br, dil=dil, nb=nb):
            work = []
            for u in range(DIL_UNROLL):
                idx = it * DIL_UNROLL + u
                r, n = idx // nb, idx % nb
                q0 = r + n * (span * dil)
                first = 1 - jnp.minimum(n, 1)
                k0 = q0 - (1 - first) * (span * dil)
                q2 = q_ref[0, rows(q0, span, dil), :] * c_hd
                k2 = k_ref[0, rows(k0, 2 * span, dil), :].astype(BF16)
                v2 = v_ref[0, rows(k0, 2 * span, dil), :]
                for hh in range(2):
                    qz = jnp.where(own(q2.shape, hh), q2, 0.0).astype(BF16)
                    work.append((hh, q0, span, v2, bias_ref[hh, 2 * br + first], _dot_nt(qz, k2)))
            finish_blocks(work, br, dil)
            return carry

        def group_whole(it, carry, br=br, dil=dil):
            work = []
            for u in range(DIL_UNROLL // 2):
                r = it * (DIL_UNROLL // 2) + u
                q2 = q_ref[0, rows(r, 2 * span, dil), :] * c_hd
                k2 = k_ref[0, rows(r, 2 * span, dil), :].astype(BF16)
                v2 = v_ref[0, rows(r, 2 * span, dil), :]
                for hh in range(2):
                    qz = jnp.where(own(q2.shape, hh), q2, 0.0).astype(BF16)
                    work.append((hh, r, 2 * span, v2, bias2_ref[hh], _dot_nt(qz, k2)))
            finish_blocks(work, br, dil)
            return carry

        if nb == 2:
            lax.fori_loop(0, dil // (DIL_UNROLL // 2), group_whole, 0)
        else:
            lax.fori_loop(0, dil * nb // DIL_UNROLL, group, 0)

    chunk = 256
    lane_c = lax.broadcasted_iota(jnp.int32, (chunk, LANES), 1)

    def finish(ci, carry):
        sl = pl.ds(ci * chunk, chunk)
        outs = []
        for hh in range(2):
            ms = [max_ref[br, hh, sl, :] for br in range(len(DIL_PAIRS))]
            mx = functools.reduce(jnp.maximum, ms)
            tot = sum(acc_ref[br, hh, sl, :] * jnp.exp2(m - mx) for br, m in enumerate(ms))
            outs.append(tot / pltpu.roll(tot, HEAD_DIM, axis=1))
        o_ref[0, sl, :] = jnp.where(lane_c < HEAD_DIM, outs[0], outs[1]).astype(BF16)
        return carry

    lax.fori_loop(0, t // chunk, finish, 0)


def _dil_bias_table():
    qi = np.arange(DIL_SPAN)[:, None]
    kj = np.arange(2 * DIL_SPAN)[None, :]
    tab = np.zeros((N_GROUP, 2 * len(DIL_PAIRS), DIL_SPAN, 2 * DIL_SPAN), np.float32)
    for variant in range(2):
        delta = qi - kj + (DIL_SPAN if variant == 0 else 0)
        valid = (delta >= 0) & (delta <= DIL_SPAN)
        for hd in range(N_GROUP):
            for br, (_, dil) in enumerate(DIL_PAIRS):
                tab[hd, 2 * br + variant] = np.where(valid, -SLOPES_DIL[hd] * LOG2E * (delta * dil), NEG)
    return tab


def _dil_bias_whole(dil):
    delta = np.arange(2 * DIL_SPAN)[:, None] - np.arange(2 * DIL_SPAN)[None, :]
    valid = (delta >= 0) & (delta <= DIL_SPAN)
    return np.stack([np.where(valid, -SLOPES_DIL[hd] * LOG2E * (delta * dil), NEG)
                     for hd in range(N_GROUP)]).astype(np.float32)


def _dil_call(yd, bias, bias2):
    b, t, _ = yd.shape
    for window, dil in DIL_PAIRS:
        nb = t // dil // DIL_SPAN
        assert window // dil == DIL_SPAN and nb * dil * DIL_SPAN == t and nb >= 2
        assert (dil * nb) % DIL_UNROLL == 0
    n_pair = N_GROUP // 2
    n_br = len(DIL_PAIRS)
    return pl.pallas_call(
        functools.partial(_dil_kernel, t=t),
        out_shape=jax.ShapeDtypeStruct((b, t, HW), BF16),
        grid=(b, n_pair),
        in_specs=[
            pl.BlockSpec((1, t, LANES), lambda bi, p: (bi, 0, p)),
            pl.BlockSpec((1, t, LANES), lambda bi, p: (bi, 0, n_pair + p)),
            pl.BlockSpec((1, t, LANES), lambda bi, p: (bi, 0, 2 * n_pair + p)),
            pl.BlockSpec((2, 2 * n_br, DIL_SPAN, 2 * DIL_SPAN), lambda bi, p: (p, 0, 0, 0)),
            pl.BlockSpec((2, 2 * DIL_SPAN, 2 * DIL_SPAN), lambda bi, p: (p, 0, 0)),
        ],
        out_specs=pl.BlockSpec((1, t, LANES), lambda bi, p: (bi, 0, p)),
        scratch_shapes=[pltpu.VMEM((n_br, 2, t, LANES), F32), pltpu.VMEM((n_br, 2, t, LANES), F32)],
        compiler_params=pltpu.CompilerParams(dimension_semantics=("arbitrary", "arbitrary"),
                                             vmem_limit_bytes=VMEM_LIMIT),
        name="dilated",
    )(yd, yd, yd, bias, bias2)


def _cols(w, spec):
    parts = []
    for item in spec:
        if isinstance(item, int):
            parts.append(jnp.zeros((w.shape[0], item), w.dtype))
        else:
            piece = w[:, item[0]:item[0] + item[1]]
            parts.append(-piece if len(item) == 3 else piece)
    return jnp.concatenate(parts, axis=1)


def _layout_specs():
    half = MLA_ROPE // 2
    pad = LANES - HEAD_DIM
    extra = np.zeros((N_GROUP + MLA_ROPE, WX_END), np.float32)
    for base in (0, HEAD_DIM):
        for grp in range(2):
            for hd in range(N_GROUP):
                c0 = WX_F + base + (grp * N_GROUP + hd) * N_PIECE
                extra[hd, c0:c0 + N_PIECE] = 1.0
    for r in range(MLA_ROPE):
        extra[N_GROUP + r, WX_KR + HEAD_DIM + r] = 1.0
        dst = r - half if r >= half else r + half
        extra[N_GROUP + r, WX_KRH + HEAD_DIM + dst] = -1.0 if r >= half else 1.0
    qw = HEAD_DIM + MLA_ROPE
    uq = [s for hd in range(N_GROUP) for s in ((hd * qw, qw), LANES - qw)]
    uqrh = [s for hd in range(N_GROUP)
            for s in (HEAD_DIM, (hd * qw + HEAD_DIM + half, half, -1.0), (hd * qw + HEAD_DIM, half), LANES - qw)]
    kn = [s for hd in range(N_GROUP) for s in ((hd * LANES, HEAD_DIM), pad)]
    vv = [(hd * LANES + HEAD_DIM, HEAD_DIM) for hd in range(N_GROUP)]
    q_in, kv_in = N_GROUP * qw, N_GROUP * LANES
    return (extra, _selection(q_in, uq), _selection(q_in, uqrh), _selection(kv_in, kn),
            _selection(kv_in, vv))


def _selection(n_in, spec):
    width = sum(item if isinstance(item, int) else item[1] for item in spec)
    sel = np.zeros((n_in, width), np.float32)
    col = 0
    for item in spec:
        if isinstance(item, int):
            col += item
            continue
        for j in range(item[1]):
            sel[item[0] + j, col + j] = -1.0 if len(item) == 3 else 1.0
        col += item[1]
    return sel


def _relayout(w, sel):
    return jnp.dot(w, jnp.asarray(sel, BF16), preferred_element_type=F32).astype(BF16)


_SPECS = _layout_specs()


def _rope_tables(t):
    inv = ROPE_THETA ** (-jnp.arange(0, MLA_ROPE, 2, dtype=F32) / MLA_ROPE)
    ang = jnp.arange(t, dtype=F32)[:, None] * inv[None, :]
    cos, sin = jnp.cos(ang), jnp.sin(ang)
    z_lo = jnp.zeros((t, HEAD_DIM), F32)
    z_hi = jnp.zeros((t, LANES - HEAD_DIM - MLA_ROPE), F32)
    cosq = jnp.concatenate([jnp.ones((t, HEAD_DIM), F32), cos, cos, z_hi], axis=1)
    sin_tab = jnp.concatenate([z_lo, sin, sin, z_hi], axis=1)
    cosk = jnp.concatenate([z_lo, cos, cos, z_hi], axis=1)
    return cosq.T, sin_tab.T, cosk, sin_tab


def kernel(x, norm_ffn1, ffn1_gate, ffn1_up, ffn1_down, norm_mix, w_in, forget_bias, mla_q_norm,
           mla_kv_norm, w_uq, w_ukv, w_out, norm_ffn2, ffn2_gate, ffn2_up, ffn2_down, norm_final):
    b, t, d = x.shape
    depth = w_in.shape[0]
    assert d == D_MODEL and t % ATT_TQ == 0 and ATT_TQ % TILE == 0 and (b * t) % FFN_TILE == 0
    extra_spec, uq_spec, uqrh_spec, kn_spec, vv_spec = _SPECS
    cosq, sinq, cosk, sink = _rope_tables(t)
    tri = jnp.tril(jnp.ones((TILE, TILE), F32)).astype(BF16)
    dil_bias = jnp.asarray(_dil_bias_table())
    whole = [dil for _, dil in DIL_PAIRS if t // dil // DIL_SPAN == 2]
    dil_bias2 = jnp.asarray(_dil_bias_whole(whole[0] if whole else 1))
    lane = np.arange(LANES)
    fg_off = lane % HEAD_DIM
    in_f = fg_off < FG_W
    fg_head = np.where(in_f, (fg_off // N_PIECE) % N_GROUP, 0)
    fg_piece = np.where(in_f, fg_off % N_PIECE, -1).astype(np.int32)
    pq, pk = jnp.asarray(fg_piece[None, :]), jnp.asarray(fg_piece[:, None])
    bf = lambda w: w.astype(BF16)
    row = lambda v: v.reshape(1, -1)

    xf = x.reshape(b * t, d)
    for l in range(depth):
        xf = _ffn_call(xf, None, None, row(norm_ffn1[l]), ffn1_gate, ffn1_up, ffn1_down, l, None)
        w_fr = bf(jnp.concatenate([w_in[l, :, BF0:BF0 + N_GROUP], w_in[l, :, CKR0:CKR0 + MLA_ROPE]], axis=1))
        wx = _relayout(w_fr, extra_spec)
        fb =jnp.where(jnp.asarray(in_f), jnp.take(forget_bias[l], jnp.asarray(fg_head)), 0.0)[None, :]
        wuq_l, wukv_l = bf(w_uq[l]), bf(w_ukv[l])
        wuqt = _relayout(wuq_l, uq_spec).T
        wuqrht = _relayout(wuq_l, uqrh_spec).T
        wkn = _relayout(wukv_l, kn_spec)
        wvt = _relayout(wukv_l, vv_spec).T
        (kr_a, qt_a0, vt_a, kr_b, qt_b, vt_b, kr_c, qt_c, vt_c, yd) = _proj_call(
            xf.reshape(b, t, d), row(norm_mix[l]), w_in, l, wx, fb, pq, pk, tri,
            row(mla_q_norm[l]), row(mla_kv_norm[l]), wuqt, wuqrht, wkn, wvt, cosq, sinq, cosk, sink)
        qt_a = _gate_call(qt_a0, kr_a)
        o_a = _attn_call(kr_a, qt_a, vt_a, "attn_moba")
        o_b = _attn_call(kr_b, qt_b, vt_b, "attn_forget")
        o_c = _attn_call(kr_c, qt_c, vt_c, "attn_mla")
        o_d = _dil_call(yd, dil_bias, dil_bias2)
        mix = [o.reshape(b * t, HW) for o in (o_a, o_b, o_c, o_d)]
        xf = _ffn_call(xf, mix, bf(w_out[l]), row(norm_ffn2[l]), ffn2_gate, ffn2_up, ffn2_down, l,
                       row(norm_final) if l == depth - 1 else None)
    return xf.reshape(b, t, d)
```

```python
import functools

import numpy as np
import jax
import jax.numpy as jnp
from jax import lax
from jax.experimental import pallas as pl
from jax.experimental.pallas import tpu as pltpu

F32 = jnp.float32
BF16 = jnp.bfloat16

D_MODEL = 1024
HEAD_DIM = 64
N_GROUP = 4
HW = N_GROUP * HEAD_DIM
LANES = 128
MOBA_BLOCK = 256
MOBA_TOPK = 3
MLA_Q_RANK = 256
MLA_KV_RANK = 128
MLA_ROPE = 32
ROPE_THETA = 10000.0
DIL_PAIRS = ((128, 1), (512, 4), (2048, 16))
DIL_SPAN = 128
D_FF = 2816
RMS_EPS = 1e-6
NEG = -1e30
LOG2E = 1.4426950408889634

N_PIECE = 3
MOBA_NBLK = 16
SEL_OFF = 0
ONE_OFF = 16
PIECE_OFF = 19
FGA_OFF = 0
FGB_OFF = N_GROUP * N_PIECE
FG_W = 2 * N_GROUP * N_PIECE

TILE = 512
ATT_TQ = 1024
LAG_LIMIT = 64.0
FFN_TILE = 512
FF_CHUNK = 256
VMEM_LIMIT = 56 * 1024 * 1024

_ALIBI = [2.0 ** (-8.0 * (i + 1) / 8) for i in range(8)]
SLOPES_MOBA = _ALIBI[1::2]
SLOPES_DIL = _ALIBI[0::2]

_WIDTHS = (HW, HW, HW, HW, HW, HW, N_GROUP, MLA_Q_RANK, MLA_KV_RANK, MLA_ROPE, HW, HW, HW)
_OFF = np.concatenate([[0], np.cumsum(_WIDTHS)]).astype(np.int64)
(AQ0, AK0, AV0, BQ0, BK0, BV0, BF0, CQ0, CKV0, CKR0, DQ0, DK0, DV0, IN_WIDTH) = [int(v) for v in _OFF]


def _dot(a, b):
    return jnp.dot(a, b, preferred_element_type=F32)


def _dot_nt(a, b):
    return lax.dot_general(a, b, (((1,), (1,)), ((), ())), preferred_element_type=F32)


def _rms(x, g):
    return x * lax.rsqrt(jnp.mean(x * x, axis=-1, keepdims=True) + RMS_EPS) * g


def _split3(x):
    p0 = x.astype(BF16)
    r = x - p0.astype(F32)
    p1 = r.astype(BF16)
    p2 = (r - p1.astype(F32)).astype(BF16)
    return p0, p1, p2


def _log_sigmoid(x):
    return jnp.minimum(x, 0.0) - jnp.log(1.0 + jnp.exp(-jnp.abs(x)))


def _ffn_kernel(*refs, has_mix, final_norm):
    it = iter(refs)
    x_ref = next(it)
    if has_mix:
        o_refs = [next(it) for _ in range(4)]
        wo_ref = next(it)
    g_ref, wg_ref, wu_ref, wd_ref = next(it), next(it), next(it), next(it)
    gf_ref = next(it) if final_norm else None
    out_ref = next(it)

    x = x_ref[...]
    if has_mix:
        for i, o_ref in enumerate(o_refs):
            x = x + _dot(o_ref[...], wo_ref[i * HW:(i + 1) * HW, :])
    h = _rms(x, g_ref[...]).astype(BF16)
    y = jnp.zeros(x.shape, F32)
    for c in range(D_FF // FF_CHUNK):
        sl = slice(c * FF_CHUNK, (c + 1) * FF_CHUNK)
        gate = _dot(h, wg_ref[:, sl].astype(BF16))
        up = _dot(h, wu_ref[:, sl].astype(BF16))
        act = (gate * jax.nn.sigmoid(gate) * up).astype(BF16)
        y = y + _dot(act, wd_ref[sl, :].astype(BF16))
    x = x + 0.5 * y
    if final_norm:
        x = _rms(x, gf_ref[...])
    out_ref[...] = x


def _const_spec(shape):
    nd = len(shape)
    return pl.BlockSpec(shape, lambda *_: (0,) * nd, pipeline_mode=pl.Buffered(1))


def _layer_spec(shape, layer):
    nd = len(shape)
    return pl.BlockSpec((None,) + tuple(shape), lambda *_: (layer,) + (0,) * nd, pipeline_mode=pl.Buffered(1))


def _ffn_call(x, mix, w_out, g, wg, wu, wd, layer, g_final):
    n = x.shape[0]
    tm = FFN_TILE
    has_mix = mix is not None
    final_norm = g_final is not None
    row = lambda i: (i, 0)
    args, specs = [x], [pl.BlockSpec((tm, D_MODEL), row)]
    if has_mix:
        for o in mix:
            args.append(o)
            specs.append(pl.BlockSpec((tm, HW), row))
        args.append(w_out)
        specs.append(_const_spec((D_MODEL, D_MODEL)))
    args += [g, wg, wu, wd]
    specs += [_const_spec((1, D_MODEL)), _layer_spec((D_MODEL, D_FF), layer), _layer_spec((D_MODEL, D_FF), layer),
              _layer_spec((D_FF, D_MODEL), layer)]
    if final_norm:
        args.append(g_final)
        specs.append(_const_spec((1, D_MODEL)))
    return pl.pallas_call(
        functools.partial(_ffn_kernel, has_mix=has_mix, final_norm=final_norm),
        out_shape=jax.ShapeDtypeStruct((n, D_MODEL), F32),
        grid=(n // tm,),
        in_specs=specs,
        out_specs=pl.BlockSpec((tm, D_MODEL), row),
        compiler_params=pltpu.CompilerParams(dimension_semantics=("arbitrary",),
                                             vmem_limit_bytes=VMEM_LIMIT),
        name="ffn_mix" if has_mix else "ffn",
    )(*args)


PAD_F = -(BF0 + N_GROUP) % LANES
PAD_R = -(CKR0 + MLA_ROPE + PAD_F) % LANES
AL_AK, AL_BK, AL_CQ = AK0, BK0, CQ0 + PAD_F
AL_CKV = CKV0 + PAD_F
AL_D = DQ0 + PAD_F + PAD_R
AL_END = IN_WIDTH + PAD_F + PAD_R
assert all(v % LANES == 0 for v in (AL_AK, AL_BK, AL_CQ, AL_CKV, AL_D, AL_END)) and AL_D - AL_CKV == 2 * LANES
WX_F, WX_KR, WX_KRH, WX_END = 0, LANES, 2 * LANES, 3 * LANES
ROPE_SLAB = CKR0 // LANES * LANES
assert BF0 % LANES == 0 and CKR0 + MLA_ROPE <= ROPE_SLAB + LANES
NT_AQ, NT_AV, NT_BQ, NT_BV, NT_END = AQ0, AV0, BQ0, BV0, BF0


def _pieces_select(idx, base, val):
    p0, p1, p2 = _split3(val)
    return jnp.where(idx == base, p0.astype(F32),
                     jnp.where(idx == base + 1, p1.astype(F32), p2.astype(F32)))


def _pieces_by(which, val):
    p0, p1, p2 = _split3(val)
    return jnp.where(which == 0, p0.astype(F32), jnp.where(which == 1, p1.astype(F32), p2.astype(F32)))


def _vt_with_ones(vt, head):
    row = lax.broadcasted_iota(jnp.int32, vt.shape, 0)
    keep = (row < HEAD_DIM) if head % 2 == 0 else (row >= HEAD_DIM)
    return jnp.where(keep, vt, 1.0).astype(BF16)


def _proj_kernel(x_ref, g_ref, w_ref, sel_ref, fb_ref, pq_ref, pk_ref, tri_ref,
                 qn_ref, kvn_ref, wuqt_ref, wuqrht_ref, wkn_ref, wvt_ref,
                 cosq_ref, sinq_ref, cosk_ref, sink_ref,
                 kr_a, qt_a, vt_a, kr_b, qt_b, vt_b, kr_c, qt_c, vt_c, yd_ref,
                 carry, wal_ref, wnt_ref, wx_ref, *, tm):
    i = pl.program_id(1)

    @pl.when((pl.program_id(0) == 0) & (i == 0))
    def _():
        for src, dst in ((AK0, AL_AK), (BK0, AL_BK)):
            wal_ref[:, dst:dst + HW] = w_ref[:, src:src + HW].astype(BF16)
        wal_ref[:, AL_CQ:AL_CQ + DQ0 - CQ0] = w_ref[:, CQ0:DQ0].astype(BF16)
        wal_ref[:, AL_D - PAD_R:AL_D] = jnp.zeros((D_MODEL, PAD_R), BF16)
        wal_ref[:, AL_D:AL_END] = w_ref[:, DQ0:IN_WIDTH].astype(BF16)
        for src in (NT_AQ, NT_AV, NT_BQ, NT_BV):
            wnt_ref[src:src + HW, :] = w_ref[:, src:src + HW].T.astype(BF16)
        slabs = jnp.concatenate([w_ref[:, BF0:BF0 + LANES], w_ref[:, ROPE_SLAB:ROPE_SLAB + LANES]], axis=1)
        wx_ref[...] = _dot(slabs.astype(BF16), sel_ref[...]).astype(BF16)

    h = _rms(x_ref[0], g_ref[...]).astype(BF16)
    lane = lax.broadcasted_iota(jnp.int32, (tm, LANES), 1)
    rowi = lax.broadcasted_iota(jnp.int32, (LANES, tm), 0)
    tok_r_i = i * tm + lax.broadcasted_iota(jnp.int32, (tm, LANES), 0)
    tok_r = tok_r_i.astype(F32)
    tok_c = (i * tm + lax.broadcasted_iota(jnp.int32, (LANES, tm), 1)).astype(F32)
    c_hd = HEAD_DIM ** -0.5 * LOG2E
    head_sl = lambda hd: slice(hd * LANES, (hd + 1) * LANES)

    @pl.when(i == 0)
    def _():
        carry[...] = jnp.zeros_like(carry)

    def feature_masks(idx, hd):
        qk0 = HEAD_DIM * (hd % 2)
        return (idx >= qk0) & (idx < qk0 + HEAD_DIM), idx - (HEAD_DIM - qk0)

    between = lambda off, lo, n: (off >= lo) & (off < lo + n)

    zk = _dot(h, wal_ref[:, AL_AK:AL_AK + HW])
    zq = _dot_nt(wnt_ref[NT_AQ:NT_AQ + HW, :], h)
    zv = _dot_nt(wnt_ref[NT_AV:NT_AV + HW, :], h)
    for hd in range(N_GROUP):
        slope2 = SLOPES_MOBA[hd] * LOG2E
        pair = head_sl(hd // 2)
        is_k, off = feature_masks(lane, hd)
        ind = between(off, SEL_OFF, MOBA_NBLK) & ((tok_r_i // MOBA_BLOCK) == off - SEL_OFF)
        ext = _pieces_select(off, ONE_OFF, tok_r * slope2)
        kr_a[0, hd] = jnp.where(is_k, zk[:, pair],
                                jnp.where(between(off, ONE_OFF, N_PIECE), ext,
                                          jnp.where(between(off, PIECE_OFF, N_PIECE) | ind, 1.0, 0.0))).astype(BF16)
        is_q, off = feature_masks(rowi, hd)
        ext = _pieces_select(off, PIECE_OFF, tok_c * (-slope2))
        qt_a[0, hd, 0] = jnp.where(is_q, zq[pair, :] * c_hd,
                                   jnp.where(between(off, ONE_OFF, N_PIECE), 1.0,
                                             jnp.where(between(off, PIECE_OFF, N_PIECE), ext, 0.0))).astype(BF16)
        vt_a[0, hd, 0] = _vt_with_ones(zv[pair, :], hd)

    zfr = _dot(h, wx_ref[:, WX_F:WX_KRH])
    in_f = (lane & (HEAD_DIM - 1)) < FG_W
    lf = jnp.where(in_f, _log_sigmoid(zfr[:, :LANES] + fb_ref[...]), 0.0)
    p0, p1, p2 = _split3(lf)
    tri = tri_ref[...]
    dec = _dot(tri, p0) + _dot(tri, p1) + _dot(tri, p2) + carry[0:1, :]
    carry[0:1, :] = dec[tm - 1:tm, :]
    e_k = _pieces_by(pq_ref[...], dec * (-LOG2E))
    e_q = _pieces_by(pk_ref[...], dec.T * LOG2E)
    zk = _dot(h, wal_ref[:, AL_BK:AL_BK + HW])
    zq = _dot_nt(wnt_ref[NT_BQ:NT_BQ + HW, :], h)
    zv = _dot_nt(wnt_ref[NT_BV:NT_BV + HW, :], h)
    for hd in range(N_GROUP):
        pair = head_sl(hd // 2)
        is_k, off = feature_masks(lane, hd)
        kr_b[0, hd] = jnp.where(is_k, zk[:, pair],
                                jnp.where(between(off, FGA_OFF, FG_W // 2), e_k,
                                          jnp.where(between(off, FGB_OFF + N_PIECE * hd, N_PIECE), 1.0, 0.0))
                                ).astype(BF16)
        is_q, off = feature_masks(rowi, hd)
        qt_b[0, hd, 0] = jnp.where(is_q, zq[pair, :] * c_hd,
                                   jnp.where(between(off, FGB_OFF, FG_W // 2), e_q,
                                             jnp.where(between(off, FGA_OFF + N_PIECE * hd, N_PIECE), 1.0, 0.0))
                                   ).astype(BF16)
        vt_b[0, hd, 0] = _vt_with_ones(zv[pair, :], hd)

    c_mla = (HEAD_DIM + MLA_ROPE) ** -0.5 * LOG2E
    cqn = _rms(_dot(h, wal_ref[:, AL_CQ:AL_CQ + MLA_Q_RANK]), qn_ref[...]).astype(BF16)
    zc = _dot(h, wal_ref[:, AL_CKV:AL_D])
    ckvn = _rms(zc[:, :MLA_KV_RANK], kvn_ref[...]).astype(BF16)
    zrh = _dot(h, wx_ref[:, WX_KRH:WX_END])
    rot_k = zfr[:, LANES:] * cosk_ref[...] + zrh * sink_ref[...]
    kn = _dot(ckvn, wkn_ref[...])
    qt = _dot_nt(wuqt_ref[...], cqn)
    qrht = _dot_nt(wuqrht_ref[...], cqn)
    vt = _dot_nt(wvt_ref[...], ckvn)
    cosq, sinq = cosq_ref[...], sinq_ref[...]
    for hd in range(N_GROUP):
        kr_c[0, hd] = (kn[:, head_sl(hd)] + rot_k).astype(BF16)
        qt_c[0, hd, 0] = ((qt[head_sl(hd), :] * cosq + qrht[head_sl(hd), :] * sinq) * c_mla).astype(BF16)
        vt_c[0, hd, 0] = _vt_with_ones(vt[head_sl(hd // 2), :], hd)

    yd_ref[0] = _dot(h, wal_ref[:, AL_D:AL_END])


def _proj_call(x3, g, w_in, layer, sel, fb, pq, pk, tri, qn, kvn, wuqt, wuqrht, wkn, wvt, cosq, sinq, cosk, sink):
    b, t, _ = x3.shape
    tm = TILE
    nk = t // tm
    per_q = ATT_TQ // tm
    kr_shape = jax.ShapeDtypeStruct((b, N_GROUP, t, LANES), BF16)
    vt_shape = jax.ShapeDtypeStruct((b, N_GROUP, nk, LANES, tm), BF16)
    qt_shape = jax.ShapeDtypeStruct((b, N_GROUP, t // ATT_TQ, LANES, ATT_TQ), BF16)
    kr_spec = pl.BlockSpec((1, N_GROUP, tm, LANES), lambda bi, i: (bi, 0, i, 0))
    vt_spec = pl.BlockSpec((1, N_GROUP, 1, LANES, tm), lambda bi, i: (bi, 0, i, 0, 0))
    qt_spec = pl.BlockSpec((1, N_GROUP, 1, LANES, tm), lambda bi, i: (bi, 0, i // per_q, 0, i % per_q))
    in_specs = [
        pl.BlockSpec((1, tm, D_MODEL), lambda bi, i: (bi, i, 0)),
        _const_spec((1, D_MODEL)),
        _layer_spec((D_MODEL, IN_WIDTH), layer),
        _const_spec((2 * LANES, WX_END)),
        _const_spec((1, LANES)),
        _const_spec((1, LANES)),
        _const_spec((LANES, 1)),
        _const_spec((tm, tm)),
        _const_spec((1, MLA_Q_RANK)),
        _const_spec((1, MLA_KV_RANK)),
        _const_spec((N_GROUP * LANES, MLA_Q_RANK)),
        _const_spec((N_GROUP * LANES, MLA_Q_RANK)),
        _const_spec((MLA_KV_RANK, N_GROUP * LANES)),
        _const_spec((HW, MLA_KV_RANK)),
        pl.BlockSpec((LANES, tm), lambda bi, i: (0, i)),
        pl.BlockSpec((LANES, tm), lambda bi, i: (0, i)),
        pl.BlockSpec((tm, LANES), lambda bi, i: (i, 0)),
        pl.BlockSpec((tm, LANES), lambda bi, i: (i, 0)),
    ]
    out_shape = [kr_shape, qt_shape, vt_shape] * 3 + [jax.ShapeDtypeStruct((b, t, 3 * HW), F32)]
    out_specs = [kr_spec, qt_spec, vt_spec] * 3 + [pl.BlockSpec((1, tm, 3 * HW), lambda bi, i: (bi, i, 0))]
    return pl.pallas_call(
        functools.partial(_proj_kernel, tm=tm),
        out_shape=out_shape,
        grid=(b, nk),
        in_specs=in_specs,
        out_specs=out_specs,
        scratch_shapes=[pltpu.VMEM((8, LANES), F32), pltpu.VMEM((D_MODEL, AL_END), BF16),
                        pltpu.VMEM((NT_END, D_MODEL), BF16), pltpu.VMEM((D_MODEL, WX_END), BF16)],
        compiler_params=pltpu.CompilerParams(dimension_semantics=("arbitrary", "arbitrary"),
                                             vmem_limit_bytes=VMEM_LIMIT),
        name="proj",
    )(x3, g, w_in, sel, fb, pq, pk, tri, qn, kvn, wuqt, wuqrht, wkn, wvt, cosq, sinq, cosk, sink)


def _gate_kernel(qt0_ref, kr_ref, qt_ref, *, t, tq):
    nq = t // tq
    nblk = t // MOBA_BLOCK
    blk = lax.broadcasted_iota(jnp.int32, (nblk, tq), 0)
    lane = lax.broadcasted_iota(jnp.int32, (nblk, LANES), 1)
    for hd in range(N_GROUP):
        ksum = jnp.zeros((nblk, LANES), F32)
        for j in range(nq):
            key_blk = (j * tq + lax.broadcasted_iota(jnp.int32, (nblk, tq), 1)) // MOBA_BLOCK
            ind = jnp.where(key_blk == blk, 1.0, 0.0).astype(BF16)
            ksum = ksum + _dot(ind, kr_ref[0, hd, j * tq:(j + 1) * tq, :])
        qk0 = HEAD_DIM * (hd % 2)
        sel0 = HEAD_DIM - qk0 + SEL_OFF
        is_k = (lane >= qk0) & (lane < qk0 + HEAD_DIM)
        kmean = jnp.where(is_k, ksum * (1.0 / MOBA_BLOCK), 0.0).astype(BF16)
        for qi in range(nq):
            q0 = qt0_ref[0, hd, qi]
            cur = (qi * tq + lax.broadcasted_iota(jnp.int32, (nblk, tq), 1)) // MOBA_BLOCK
            valid = blk < cur
            gate = jnp.where(valid, _dot(kmean, q0), -jnp.inf)
            rank = jnp.zeros((nblk, tq), F32)
            for m in range(nblk):
                gm = gate[m:m + 1, :]
                ahead = (gm > gate) | ((gm == gate) & (blk > m))
                rank = rank + jnp.where(ahead, 1.0, 0.0)
            keep = (valid & (rank < MOBA_TOPK)) | (blk == cur)
            selb = jnp.where(keep, 0.0, NEG).astype(BF16)
            parts = ([q0[:sel0]] if sel0 else []) + [selb, q0[sel0 + nblk:]]
            qt_ref[0, hd, qi] = jnp.concatenate(parts, axis=0)


def _gate_call(qt0, kr):
    b, _, nq, _, tq = qt0.shape
    t = nq * tq
    nblk = t // MOBA_BLOCK
    assert nblk == MOBA_NBLK and SEL_OFF % 16 == 0
    qt_spec = pl.BlockSpec((1, N_GROUP, nq, LANES, tq), lambda bi: (bi, 0, 0, 0, 0))
    kr_spec = pl.BlockSpec((1, N_GROUP, t, LANES), lambda bi: (bi, 0, 0, 0))
    return pl.pallas_call(
        functools.partial(_gate_kernel, t=t, tq=tq),
        out_shape=jax.ShapeDtypeStruct(qt0.shape, BF16),
        grid=(b,),
        in_specs=[qt_spec, kr_spec],
        out_specs=qt_spec,
        compiler_params=pltpu.CompilerParams(dimension_semantics=("arbitrary",),
                                             vmem_limit_bytes=VMEM_LIMIT),
        name="moba_gate",
    )(qt0, kr)


def _attn_kernel(kr_ref, qt_ref, vt_ref, o_ref, acc_ref, *, tq, tk):
    qi = pl.program_id(1)
    ratio = tq // tk
    rowi = lax.broadcasted_iota(jnp.int32, (LANES, tq), 0)
    acc_ref[...] = jnp.zeros(acc_ref.shape, F32)

    def step(j, ms, c0, masked):
        w = tq - c0
        k0 = pl.multiple_of(j * tk, tk)
        sts = [_dot(kr_ref[0, hd, pl.ds(k0, tk), :], qt_ref[0, hd, 0, :, c0:]) for hd in range(N_GROUP)]
        out = []
        for hd in range(N_GROUP):
            st = sts[hd]
            if masked:
                keyi = lax.broadcasted_iota(jnp.int32, (tk, w), 0)
                qryi = lax.broadcasted_iota(jnp.int32, (tk, w), 1)
                st = jnp.where(keyi <= qryi, st, NEG)
            m_old = ms[hd][:, c0:]
            m_new = jnp.maximum(m_old, jnp.max(st, axis=0, keepdims=True))
            p = jnp.exp2(st - m_new)
            pv = _dot(vt_ref[0, hd, j], p.astype(BF16))
            acc_ref[hd, :, c0:] = jnp.exp2(m_old - m_new) * acc_ref[hd, :, c0:] + pv
            out.append(m_new if c0 == 0 else jnp.concatenate([ms[hd][:, :c0], m_new], axis=1))
        return tuple(out)

    def lagged_step(j, carry):
        used, nxt, excess = carry
        k0 = pl.multiple_of(j * tk, tk)
        sts = [_dot(kr_ref[0, hd, pl.ds(k0, tk), :], qt_ref[0, hd, 0]) for hd in range(N_GROUP)]
        new_nxt = []
        for hd in range(N_GROUP):
            st, ref = sts[hd], nxt[hd]
            p = jnp.exp2(st - ref)
            block_max = jnp.max(st, axis=0, keepdims=True)
            pv = _dot(vt_ref[0, hd, j], p.astype(BF16))
            acc_ref[hd] = jnp.exp2(used[hd] - ref) * acc_ref[hd] + pv
            excess = jnp.maximum(excess, block_max - ref)
            new_nxt.append(jnp.maximum(ref, block_max))
        return nxt, tuple(new_nxt), excess

    def diagonal(ms):
        for d in range(ratio):
            ms = step(qi * ratio + d, ms, d * tk, True)
        return ms

    m0 = tuple(jnp.full((1, tq), -jnp.inf, F32) for _ in range(N_GROUP))
    ms = diagonal(m0)
    _, _, excess = lax.fori_loop(0, qi * ratio, lagged_step, (ms, ms, jnp.full((1, tq), -jnp.inf, F32)))

    @pl.when(jnp.max(excess) > LAG_LIMIT)
    def _():
        acc_ref[...] = jnp.zeros(acc_ref.shape, F32)
        diagonal(lax.fori_loop(0, qi * ratio, lambda j, c: step(j, c, 0, False), m0))

    for pair in range(N_GROUP // 2):
        even, odd = acc_ref[2 * pair], acc_ref[2 * pair + 1]
        out_t = jnp.where(rowi < HEAD_DIM, even / even[HEAD_DIM:HEAD_DIM + 1, :], odd / odd[0:1, :])
        o_ref[0, :, pair * LANES:(pair + 1) * LANES] = out_t.T.astype(BF16)


def _attn_call(kr, qt, vt, name):
    b, _, t, _ = kr.shape
    tq, tk = qt.shape[-1], vt.shape[-1]
    return pl.pallas_call(
        functools.partial(_attn_kernel, tq=tq, tk=tk),
        out_shape=jax.ShapeDtypeStruct((b, t, HW), BF16),
        grid=(b, t // tq),
        in_specs=[
            pl.BlockSpec((1, N_GROUP, t, LANES), lambda bi, i: (bi, 0, 0, 0)),
            pl.BlockSpec((1, N_GROUP, 1, LANES, tq), lambda bi, i: (bi, 0, i, 0, 0)),
            pl.BlockSpec((1, N_GROUP, t // tk, LANES, tk), lambda bi, i: (bi, 0, 0, 0, 0)),
        ],
        out_specs=pl.BlockSpec((1, tq, HW), lambda bi, i: (bi, i, 0)),
        scratch_shapes=[pltpu.VMEM((N_GROUP, LANES, tq), F32)],
        compiler_params=pltpu.CompilerParams(dimension_semantics=("arbitrary", "arbitrary"),
                                             vmem_limit_bytes=VMEM_LIMIT),
        name=name,
    )(kr, qt, vt)


DIL_UNROLL = 16


def _dil_kernel(q_ref, k_ref, v_ref, bias_ref, bias2_ref, o_ref, acc_ref, max_ref, *, t):
    span = DIL_SPAN
    c_hd = HEAD_DIM ** -0.5 * LOG2E

    def rows(start, size, dil):
        return pl.ds(start, size) if dil == 1 else pl.ds(start, size, stride=dil)

    def own(shape, hh):
        lane = lax.broadcasted_iota(jnp.int32, shape, 1)
        return (lane < HEAD_DIM) if hh == 0 else (lane >= HEAD_DIM)

    def finish_blocks(work, br, dil):
        for hh, q0, nq, v2, bias, qk in work:
            va = jnp.where(own(v2.shape, hh), v2, 1.0).astype(BF16)
            s = qk + bias
            m = jnp.max(s, axis=1, keepdims=True)
            p = jnp.exp2(s - m)
            acc_ref[br, hh, rows(q0, nq, dil), :] = _dot(p.astype(BF16), va)
            max_ref[br, hh, rows(q0, nq, dil), :] = jnp.broadcast_to(m, (nq, LANES))

    for br, (window, dil) in enumerate(DIL_PAIRS):
        nb = t // dil // span

        def group(it, carry, br=br, dil=dil, nb=nb):
            work = []
            for u in range(DIL_UNROLL):
                idx = it * DIL_UNROLL + u
                r, n = idx // nb, idx % nb
                q0 = r + n * (span * dil)
                first = 1 - jnp.minimum(n, 1)
                k0 = q0 - (1 - first) * (span * dil)
                q2 = q_ref[0, rows(q0, span, dil), :] * c_hd
                k2 = k_ref[0, rows(k0, 2 * span, dil), :].astype(BF16)
                v2 = v_ref[0, rows(k0, 2 * span, dil), :]
                for hh in range(2):
                    qz = jnp.where(own(q2.shape, hh), q2, 0.0).astype(BF16)
                    work.append((hh, q0, span, v2, bias_ref[hh, 2 * br + first], _dot_nt(qz, k2)))
            finish_blocks(work, br, dil)
            return carry

        def group_whole(it, carry, br=br, dil=dil):
            work = []
            for u in range(DIL_UNROLL // 2):
                r = it * (DIL_UNROLL // 2) + u
                q2 = q_ref[0, rows(r, 2 * span, dil), :] * c_hd
                k2 = k_ref[0, rows(r, 2 * span, dil), :].astype(BF16)
                v2 = v_ref[0, rows(r, 2 * span, dil), :]
                for hh in range(2):
                    qz = jnp.where(own(q2.shape, hh), q2, 0.0).astype(BF16)
                    work.append((hh, r, 2 * span, v2, bias2_ref[hh], _dot_nt(qz, k2)))
            finish_blocks(work, br, dil)
            return carry

        if nb == 2:
            lax.fori_loop(0, dil // (DIL_UNROLL // 2), group_whole, 0)
        else:
            lax.fori_loop(0, dil * nb // DIL_UNROLL, group, 0)

    chunk = 256
    lane_c = lax.broadcasted_iota(jnp.int32, (chunk, LANES), 1)

    def finish(ci, carry):
        sl = pl.ds(ci * chunk, chunk)
        outs = []
        for hh in range(2):
            ms = [max_ref[br, hh, sl, :] for br in range(len(DIL_PAIRS))]
            mx = functools.reduce(jnp.maximum, ms)
            tot = sum(acc_ref[br, hh, sl, :] * jnp.exp2(m - mx) for br, m in enumerate(ms))
            outs.append(tot / pltpu.roll(tot, HEAD_DIM, axis=1))
        o_ref[0, sl, :] = jnp.where(lane_c < HEAD_DIM, outs[0], outs[1]).astype(BF16)
        return carry

    lax.fori_loop(0, t // chunk, finish, 0)


def _dil_bias_table():
    qi = np.arange(DIL_SPAN)[:, None]
    kj = np.arange(2 * DIL_SPAN)[None, :]
    tab = np.zeros((N_GROUP, 2 * len(DIL_PAIRS), DIL_SPAN, 2 * DIL_SPAN), np.float32)
    for variant in range(2):
        delta = qi - kj + (DIL_SPAN if variant == 0 else 0)
        valid = (delta >= 0) & (delta <= DIL_SPAN)
        for hd in range(N_GROUP):
            for br, (_, dil) in enumerate(DIL_PAIRS):
                tab[hd, 2 * br + variant] = np.where(valid, -SLOPES_DIL[hd] * LOG2E * (delta * dil), NEG)
    return tab


def _dil_bias_whole(dil):
    delta = np.arange(2 * DIL_SPAN)[:, None] - np.arange(2 * DIL_SPAN)[None, :]
    valid = (delta >= 0) & (delta <= DIL_SPAN)
    return np.stack([np.where(valid, -SLOPES_DIL[hd] * LOG2E * (delta * dil), NEG)
                     for hd in range(N_GROUP)]).astype(np.float32)


def _dil_call(yd, bias, bias2):
    b, t, _ = yd.shape
    for window, dil in DIL_PAIRS:
        nb = t // dil // DIL_SPAN
        assert window // dil == DIL_SPAN and nb * dil * DIL_SPAN == t and nb >= 2
        assert (dil * nb) % DIL_UNROLL == 0
    n_pair = N_GROUP // 2
    n_br = len(DIL_PAIRS)
    return pl.pallas_call(
        functools.partial(_dil_kernel, t=t),
        out_shape=jax.ShapeDtypeStruct((b, t, HW), BF16),
        grid=(b, n_pair),
        in_specs=[
            pl.BlockSpec((1, t, LANES), lambda bi, p: (bi, 0, p)),
            pl.BlockSpec((1, t, LANES), lambda bi, p: (bi, 0, n_pair + p)),
            pl.BlockSpec((1, t, LANES), lambda bi, p: (bi, 0, 2 * n_pair + p)),
            pl.BlockSpec((2, 2 * n_br, DIL_SPAN, 2 * DIL_SPAN), lambda bi, p: (p, 0, 0, 0)),
            pl.BlockSpec((2, 2 * DIL_SPAN, 2 * DIL_SPAN), lambda bi, p: (p, 0, 0)),
        ],
        out_specs=pl.BlockSpec((1, t, LANES), lambda bi, p: (bi, 0, p)),
        scratch_shapes=[pltpu.VMEM((n_br, 2, t, LANES), F32), pltpu.VMEM((n_br, 2, t, LANES), F32)],
        compiler_params=pltpu.CompilerParams(dimension_semantics=("arbitrary", "arbitrary"),
                                             vmem_limit_bytes=VMEM_LIMIT),
        name="dilated",
    )(yd, yd, yd, bias, bias2)


def _cols(w, spec):
    parts = []
    for item in spec:
        if isinstance(item, int):
            parts.append(jnp.zeros((w.shape[0], item), w.dtype))
        else:
            piece = w[:, item[0]:item[0] + item[1]]
            parts.append(-piece if len(item) == 3 else piece)
    return jnp.concatenate(parts, axis=1)


def _layout_specs():
    half = MLA_ROPE // 2
    pad = LANES - HEAD_DIM
    extra = np.zeros((2 * LANES, WX_END), np.float32)
    for base in (0, HEAD_DIM):
        for grp in range(2):
            for hd in range(N_GROUP):
                c0 = WX_F + base + (grp * N_GROUP + hd) * N_PIECE
                extra[hd, c0:c0 + N_PIECE] = 1.0
    for r in range(MLA_ROPE):
        src = LANES + CKR0 - ROPE_SLAB + r
        extra[src, WX_KR + HEAD_DIM + r] = 1.0
        dst = r - half if r >= half else r + half
        extra[src, WX_KRH + HEAD_DIM + dst] = -1.0 if r >= half else 1.0
    qw = HEAD_DIM + MLA_ROPE
    uq = [s for hd in range(N_GROUP) for s in ((hd * qw, qw), LANES - qw)]
    uqrh = [s for hd in range(N_GROUP)
            for s in (HEAD_DIM, (hd * qw + HEAD_DIM + half, half, -1.0), (hd * qw + HEAD_DIM, half), LANES - qw)]
    kn = [s for hd in range(N_GROUP) for s in ((hd * LANES, HEAD_DIM), pad)]
    vv = [(hd * LANES + HEAD_DIM, HEAD_DIM) for hd in range(N_GROUP)]
    q_in, kv_in = N_GROUP * qw, N_GROUP * LANES
    return (extra, _selection(q_in, uq), _selection(q_in, uqrh), _selection(kv_in, kn),
            _selection(kv_in, vv))


def _selection(n_in, spec):
    width = sum(item if isinstance(item, int) else item[1] for item in spec)
    sel = np.zeros((n_in, width), np.float32)
    col = 0
    for item in spec:
        if isinstance(item, int):
            col += item
            continue
        for j in range(item[1]):
            sel[item[0] + j, col + j] = -1.0 if len(item) == 3 else 1.0
        col += item[1]
    return sel


def _relayout(w, sel):
    return jnp.dot(w, jnp.asarray(sel, BF16), preferred_element_type=F32).astype(BF16)


_SPECS = _layout_specs()


def _rope_tables(t):
    inv = ROPE_THETA ** (-jnp.arange(0, MLA_ROPE, 2, dtype=F32) / MLA_ROPE)
    ang = jnp.arange(t, dtype=F32)[:, None] * inv[None, :]
    cos, sin = jnp.cos(ang), jnp.sin(ang)
    z_lo = jnp.zeros((t, HEAD_DIM), F32)
    z_hi = jnp.zeros((t, LANES - HEAD_DIM - MLA_ROPE), F32)
    cosq = jnp.concatenate([jnp.ones((t, HEAD_DIM), F32), cos, cos, z_hi], axis=1)
    sin_tab = jnp.concatenate([z_lo, sin, sin, z_hi], axis=1)
    cosk = jnp.concatenate([z_lo, cos, cos, z_hi], axis=1)
    return cosq.T, sin_tab.T, cosk, sin_tab


def kernel(x, norm_ffn1, ffn1_gate, ffn1_up, ffn1_down, norm_mix, w_in, forget_bias, mla_q_norm,
           mla_kv_norm, w_uq, w_ukv, w_out, norm_ffn2, ffn2_gate, ffn2_up, ffn2_down, norm_final):
    b, t, d = x.shape
    depth = w_in.shape[0]
    assert d == D_MODEL and t % ATT_TQ == 0 and ATT_TQ % TILE == 0 and (b * t) % FFN_TILE == 0
    extra_spec, uq_spec, uqrh_spec, kn_spec, vv_spec = _SPECS
    extra_sel = jnp.asarray(extra_spec, BF16)
    cosq, sinq, cosk, sink = _rope_tables(t)
    tri = jnp.tril(jnp.ones((TILE, TILE), F32)).astype(BF16)
    dil_bias = jnp.asarray(_dil_bias_table())
    whole = [dil for _, dil in DIL_PAIRS if t // dil // DIL_SPAN == 2]
    dil_bias2 = jnp.asarray(_dil_bias_whole(whole[0] if whole else 1))
    lane = np.arange(LANES)
    fg_off = lane % HEAD_DIM
    in_f = fg_off < FG_W
    fg_head = np.where(in_f, (fg_off // N_PIECE) % N_GROUP, 0)
    fg_piece = np.where(in_f, fg_off % N_PIECE, -1).astype(np.int32)
    pq, pk = jnp.asarray(fg_piece[None, :]), jnp.asarray(fg_piece[:, None])
    bf = lambda w: w.astype(BF16)
    row = lambda v: v.reshape(1, -1)

    xf = x.reshape(b * t, d)
    for l in range(depth):
        xf = _ffn_call(xf, None, None, row(norm_ffn1[l]), ffn1_gate, ffn1_up, ffn1_down, l, None)
        fb =jnp.where(jnp.asarray(in_f), jnp.take(forget_bias[l], jnp.asarray(fg_head)), 0.0)[None, :]
        wuq_l, wukv_l = bf(w_uq[l]), bf(w_ukv[l])
        wuqt = _relayout(wuq_l, uq_spec).T
        wuqrht = _relayout(wuq_l, uqrh_spec).T
        wkn = _relayout(wukv_l, kn_spec)
        wvt = _relayout(wukv_l, vv_spec).T
        (kr_a, qt_a0, vt_a, kr_b, qt_b, vt_b, kr_c, qt_c, vt_c, yd) = _proj_call(
            xf.reshape(b, t, d), row(norm_mix[l]), w_in, l, extra_sel, fb, pq, pk, tri,
            row(mla_q_norm[l]), row(mla_kv_norm[l]), wuqt, wuqrht, wkn, wvt, cosq, sinq, cosk, sink)
        qt_a = _gate_call(qt_a0, kr_a)
        o_a = _attn_call(kr_a, qt_a, vt_a, "attn_moba")
        o_b = _attn_call(kr_b, qt_b, vt_b, "attn_forget")
        o_c = _attn_call(kr_c, qt_c, vt_c, "attn_mla")
        o_d = _dil_call(yd, dil_bias, dil_bias2)
        mix = [o.reshape(b * t, HW) for o in (o_a, o_b, o_c, o_d)]
        xf = _ffn_call(xf, mix, bf(w_out[l]), row(norm_ffn2[l]), ffn2_gate, ffn2_up, ffn2_down, l,
                       row(norm_final) if l == depth - 1 else None)
    return xf.reshape(b, t, d)
```

```python
import functools

import numpy as np
import jax
import jax.numpy as jnp
from jax import lax
from jax.experimental import pallas as pl
from jax.experimental.pallas import tpu as pltpu

F32 = jnp.float32
BF16 = jnp.bfloat16

D_MODEL = 1024
HEAD_DIM = 64
N_GROUP = 4
HW = N_GROUP * HEAD_DIM
LANES = 128
MOBA_BLOCK = 256
MOBA_TOPK = 3
MLA_Q_RANK = 256
MLA_KV_RANK = 128
MLA_ROPE = 32
ROPE_THETA = 10000.0
DIL_PAIRS = ((128, 1), (512, 4), (2048, 16))
DIL_SPAN = 128
D_FF = 2816
RMS_EPS = 1e-6
NEG = -1e30
LOG2E = 1.4426950408889634

N_PIECE = 3
MOBA_NBLK = 16
SEL_OFF = 0
ONE_OFF = 16
PIECE_OFF = 19
FGA_OFF = 0
FGB_OFF = N_GROUP * N_PIECE
FG_W = 2 * N_GROUP * N_PIECE

TILE = 512
ATT_TQ = 1024
LAG_LIMIT = 64.0
FFN_TILE = 512
FF_CHUNK = 256
VMEM_LIMIT = 56 * 1024 * 1024

_ALIBI = [2.0 ** (-8.0 * (i + 1) / 8) for i in range(8)]
SLOPES_MOBA = _ALIBI[1::2]
SLOPES_DIL = _ALIBI[0::2]

_WIDTHS = (HW, HW, HW, HW, HW, HW, N_GROUP, MLA_Q_RANK, MLA_KV_RANK, MLA_ROPE, HW, HW, HW)
_OFF = np.concatenate([[0], np.cumsum(_WIDTHS)]).astype(np.int64)
(AQ0, AK0, AV0, BQ0, BK0, BV0, BF0, CQ0, CKV0, CKR0, DQ0, DK0, DV0, IN_WIDTH) = [int(v) for v in _OFF]


def _dot(a, b):
    return jnp.dot(a, b, preferred_element_type=F32)


def _dot_nt(a, b):
    return lax.dot_general(a, b, (((1,), (1,)), ((), ())), preferred_element_type=F32)


def _rms(x, g):
    return x * lax.rsqrt(jnp.mean(x * x, axis=-1, keepdims=True) + RMS_EPS) * g


def _split3(x):
    p0 = x.astype(BF16)
    r = x - p0.astype(F32)
    p1 = r.astype(BF16)
    p2 = (r - p1.astype(F32)).astype(BF16)
    return p0, p1, p2


def _log_sigmoid(x):
    return jnp.minimum(x, 0.0) - jnp.log(1.0 + jnp.exp(-jnp.abs(x)))


def _ffn_kernel(*refs, has_mix, final_norm):
    it = iter(refs)
    x_ref = next(it)
    if has_mix:
        o_refs = [next(it) for _ in range(4)]
        wo_ref = next(it)
    g_ref, wg_ref, wu_ref, wd_ref = next(it), next(it), next(it), next(it)
    gf_ref = next(it) if final_norm else None
    out_ref = next(it)

    x = x_ref[...]
    if has_mix:
        for i, o_ref in enumerate(o_refs):
            x = x + _dot(o_ref[...], wo_ref[i * HW:(i + 1) * HW, :])
    h = _rms(x, g_ref[...]).astype(BF16)
    y = jnp.zeros(x.shape, F32)
    for c in range(D_FF // FF_CHUNK):
        sl = slice(c * FF_CHUNK, (c + 1) * FF_CHUNK)
        gate = _dot(h, wg_ref[:, sl].astype(BF16))
        up = _dot(h, wu_ref[:, sl].astype(BF16))
        act = (gate * jax.nn.sigmoid(gate) * up).astype(BF16)
        y = y + _dot(act, wd_ref[sl, :].astype(BF16))
    x = x + 0.5 * y
    if final_norm:
        x = _rms(x, gf_ref[...])
    out_ref[...] = x


def _const_spec(shape):
    nd = len(shape)
    return pl.BlockSpec(shape, lambda *_: (0,) * nd, pipeline_mode=pl.Buffered(1))


def _layer_spec(shape, layer):
    nd = len(shape)
    return pl.BlockSpec((None,) + tuple(shape), lambda *_: (layer,) + (0,) * nd, pipeline_mode=pl.Buffered(1))


def _ffn_call(x, mix, w_out, g, wg, wu, wd, layer, g_final):
    n = x.shape[0]
    tm = FFN_TILE
    has_mix = mix is not None
    final_norm = g_final is not None
    row = lambda i: (i, 0)
    args, specs = [x], [pl.BlockSpec((tm, D_MODEL), row)]
    if has_mix:
        for o in mix:
            args.append(o)
            specs.append(pl.BlockSpec((tm, HW), row))
        args.append(w_out)
        specs.append(_const_spec((D_MODEL, D_MODEL)))
    args += [g, wg, wu, wd]
    specs += [_const_spec((1, D_MODEL)), _layer_spec((D_MODEL, D_FF), layer), _layer_spec((D_MODEL, D_FF), layer),
              _layer_spec((D_FF, D_MODEL), layer)]
    if final_norm:
        args.append(g_final)
        specs.append(_const_spec((1, D_MODEL)))
    return pl.pallas_call(
        functools.partial(_ffn_kernel, has_mix=has_mix, final_norm=final_norm),
        out_shape=jax.ShapeDtypeStruct((n, D_MODEL), F32),
        grid=(n // tm,),
        in_specs=specs,
        out_specs=pl.BlockSpec((tm, D_MODEL), row),
        compiler_params=pltpu.CompilerParams(dimension_semantics=("arbitrary",),
                                             vmem_limit_bytes=VMEM_LIMIT),
        name="ffn_mix" if has_mix else "ffn",
    )(*args)


PAD_F = -(BF0 + N_GROUP) % LANES
PAD_R = -(CKR0 + MLA_ROPE + PAD_F) % LANES
AL_AK, AL_BK, AL_CQ = AK0, BK0, CQ0 + PAD_F
AL_CKV = CKV0 + PAD_F
AL_D = DQ0 + PAD_F + PAD_R
AL_END = IN_WIDTH + PAD_F + PAD_R
assert all(v % LANES == 0 for v in (AL_AK, AL_BK, AL_CQ, AL_CKV, AL_D, AL_END)) and AL_D - AL_CKV == 2 * LANES
WX_F, WX_KR, WX_KRH, WX_END = 0, LANES, 2 * LANES, 3 * LANES
ROPE_SLAB = CKR0 // LANES * LANES
assert BF0 % LANES == 0 and CKR0 + MLA_ROPE <= ROPE_SLAB + LANES
NT_AQ, NT_AV, NT_BQ, NT_BV, NT_END = AQ0, AV0, BQ0, BV0, BF0


def _pieces_select(idx, base, val):
    p0, p1, p2 = _split3(val)
    return jnp.where(idx == base, p0.astype(F32),
                     jnp.where(idx == base + 1, p1.astype(F32), p2.astype(F32)))


def _pieces_by(which, val):
    p0, p1, p2 = _split3(val)
    return jnp.where(which == 0, p0.astype(F32), jnp.where(which == 1, p1.astype(F32), p2.astype(F32)))


def _vt_with_ones(vt, head):
    row = lax.broadcasted_iota(jnp.int32, vt.shape, 0)
    keep = (row < HEAD_DIM) if head % 2 == 0 else (row >= HEAD_DIM)
    return jnp.where(keep, vt, 1.0).astype(BF16)


def _proj_kernel(x_ref, g_ref, w_ref, sel_ref, fb_ref, pq_ref, pk_ref, tri_ref,
                 qn_ref, kvn_ref, wuqt_ref, wuqrht_ref, wkn_ref, wvt_ref,
                 cosq_ref, sinq_ref, cosk_ref, sink_ref,
                 kr_a, qt_a, vt_a, kr_b, qt_b, vt_b, kr_c, qt_c, vt_c, yd_ref,
                 carry, wal_ref, wnt_ref, wx_ref, *, tm):
    i = pl.program_id(1)

    @pl.when((pl.program_id(0) == 0) & (i == 0))
    def _():
        for src, dst in ((AK0, AL_AK), (BK0, AL_BK)):
            wal_ref[:, dst:dst + HW] = w_ref[:, src:src + HW].astype(BF16)
        wal_ref[:, AL_CQ:AL_CQ + DQ0 - CQ0] = w_ref[:, CQ0:DQ0].astype(BF16)
        wal_ref[:, AL_D - PAD_R:AL_D] = jnp.zeros((D_MODEL, PAD_R), BF16)
        wal_ref[:, AL_D:AL_END] = w_ref[:, DQ0:IN_WIDTH].astype(BF16)
        for src in (NT_AQ, NT_AV, NT_BQ, NT_BV):
            wnt_ref[src:src + HW, :] = w_ref[:, src:src + HW].T.astype(BF16)
        slabs = jnp.concatenate([w_ref[:, BF0:BF0 + LANES], w_ref[:, ROPE_SLAB:ROPE_SLAB + LANES]], axis=1)
        wx_ref[...] = _dot(slabs.astype(BF16), sel_ref[...]).astype(BF16)

    h = _rms(x_ref[0], g_ref[...]).astype(BF16)
    lane = lax.broadcasted_iota(jnp.int32, (tm, LANES), 1)
    rowi = lax.broadcasted_iota(jnp.int32, (LANES, tm), 0)
    tok_r_i = i * tm + lax.broadcasted_iota(jnp.int32, (tm, LANES), 0)
    tok_r = tok_r_i.astype(F32)
    tok_c = (i * tm + lax.broadcasted_iota(jnp.int32, (LANES, tm), 1)).astype(F32)
    c_hd = HEAD_DIM ** -0.5 * LOG2E
    head_sl = lambda hd: slice(hd * LANES, (hd + 1) * LANES)

    @pl.when(i == 0)
    def _():
        carry[...] = jnp.zeros_like(carry)

    def feature_masks(idx, hd):
        qk0 = HEAD_DIM * (hd % 2)
        return (idx >= qk0) & (idx < qk0 + HEAD_DIM), idx - (HEAD_DIM - qk0)

    between = lambda off, lo, n: (off >= lo) & (off < lo + n)

    zk = _dot(h, wal_ref[:, AL_AK:AL_AK + HW])
    zq = _dot_nt(wnt_ref[NT_AQ:NT_AQ + HW, :], h)
    zv = _dot_nt(wnt_ref[NT_AV:NT_AV + HW, :], h)
    for hd in range(N_GROUP):
        slope2 = SLOPES_MOBA[hd] * LOG2E
        pair = head_sl(hd // 2)
        is_k, off = feature_masks(lane, hd)
        ind = between(off, SEL_OFF, MOBA_NBLK) & ((tok_r_i // MOBA_BLOCK) == off - SEL_OFF)
        ext = _pieces_select(off, ONE_OFF, tok_r * slope2)
        kr_a[0, hd] = jnp.where(is_k, zk[:, pair],
                                jnp.where(between(off, ONE_OFF, N_PIECE), ext,
                                          jnp.where(between(off, PIECE_OFF, N_PIECE) | ind, 1.0, 0.0))).astype(BF16)
        is_q, off = feature_masks(rowi, hd)
        ext = _pieces_select(off, PIECE_OFF, tok_c * (-slope2))
        qt_a[0, hd, 0] = jnp.where(is_q, zq[pair, :] * c_hd,
                                   jnp.where(between(off, ONE_OFF, N_PIECE), 1.0,
                                             jnp.where(between(off, PIECE_OFF, N_PIECE), ext, 0.0))).astype(BF16)
        vt_a[0, hd, 0] = _vt_with_ones(zv[pair, :], hd)

    zfr = _dot(h, wx_ref[:, WX_F:WX_KRH])
    in_f = (lane & (HEAD_DIM - 1)) < FG_W
    lf = jnp.where(in_f, _log_sigmoid(zfr[:, :LANES] + fb_ref[...]), 0.0)
    p0, p1, p2 = _split3(lf)
    tri = tri_ref[...]
    dec = _dot(tri, p0) + _dot(tri, p1) + _dot(tri, p2) + carry[0:1, :]
    carry[0:1, :] = dec[tm - 1:tm, :]
    e_k = _pieces_by(pq_ref[...], dec * (-LOG2E))
    e_q = _pieces_by(pk_ref[...], dec.T * LOG2E)
    zk = _dot(h, wal_ref[:, AL_BK:AL_BK + HW])
    zq = _dot_nt(wnt_ref[NT_BQ:NT_BQ + HW, :], h)
    zv = _dot_nt(wnt_ref[NT_BV:NT_BV + HW, :], h)
    for hd in range(N_GROUP):
        pair = head_sl(hd // 2)
        is_k, off = feature_masks(lane, hd)
        kr_b[0, hd] = jnp.where(is_k, zk[:, pair],
                                jnp.where(between(off, FGA_OFF, FG_W // 2), e_k,
                                          jnp.where(between(off, FGB_OFF + N_PIECE * hd, N_PIECE), 1.0, 0.0))
                                ).astype(BF16)
        is_q, off = feature_masks(rowi, hd)
        qt_b[0, hd, 0] = jnp.where(is_q, zq[pair, :] * c_hd,
                                   jnp.where(between(off, FGB_OFF, FG_W // 2), e_q,
                                             jnp.where(between(off, FGA_OFF + N_PIECE * hd, N_PIECE), 1.0, 0.0))
                                   ).astype(BF16)
        vt_b[0, hd, 0] = _vt_with_ones(zv[pair, :], hd)

    c_mla = (HEAD_DIM + MLA_ROPE) ** -0.5 * LOG2E
    cqn = _rms(_dot(h, wal_ref[:, AL_CQ:AL_CQ + MLA_Q_RANK]), qn_ref[...]).astype(BF16)
    zc = _dot(h, wal_ref[:, AL_CKV:AL_D])
    ckvn = _rms(zc[:, :MLA_KV_RANK], kvn_ref[...]).astype(BF16)
    zrh = _dot(h, wx_ref[:, WX_KRH:WX_END])
    rot_k = zfr[:, LANES:] * cosk_ref[...] + zrh * sink_ref[...]
    kn = _dot(ckvn, wkn_ref[...])
    qt = _dot_nt(wuqt_ref[...], cqn)
    qrht = _dot_nt(wuqrht_ref[...], cqn)
    vt = _dot_nt(wvt_ref[...], ckvn)
    cosq, sinq = cosq_ref[...], sinq_ref[...]
    for hd in range(N_GROUP):
        kr_c[0, hd] = (kn[:, head_sl(hd)] + rot_k).astype(BF16)
        qt_c[0, hd, 0] = ((qt[head_sl(hd), :] * cosq + qrht[head_sl(hd), :] * sinq) * c_mla).astype(BF16)
        vt_c[0, hd, 0] = _vt_with_ones(vt[head_sl(hd // 2), :], hd)

    yd_ref[0] = _dot(h, wal_ref[:, AL_D:AL_END])


def _proj_call(x3, g, w_in, layer, sel, fb, pq, pk, tri, qn, kvn, wuqt, wuqrht, wkn, wvt, cosq, sinq, cosk, sink):
    b, t, _ = x3.shape
    tm = TILE
    nk = t // tm
    per_q = ATT_TQ // tm
    kr_shape = jax.ShapeDtypeStruct((b, N_GROUP, t, LANES), BF16)
    vt_shape = jax.ShapeDtypeStruct((b, N_GROUP, nk, LANES, tm), BF16)
    qt_shape = jax.ShapeDtypeStruct((b, N_GROUP, t // ATT_TQ, LANES, ATT_TQ), BF16)
    kr_spec = pl.BlockSpec((1, N_GROUP, tm, LANES), lambda bi, i: (bi, 0, i, 0))
    vt_spec = pl.BlockSpec((1, N_GROUP, 1, LANES, tm), lambda bi, i: (bi, 0, i, 0, 0))
    qt_spec = pl.BlockSpec((1, N_GROUP, 1, LANES, tm), lambda bi, i: (bi, 0, i // per_q, 0, i % per_q))
    in_specs = [
        pl.BlockSpec((1, tm, D_MODEL), lambda bi, i: (bi, i, 0)),
        _const_spec((1, D_MODEL)),
        _layer_spec((D_MODEL, IN_WIDTH), layer),
        _const_spec((2 * LANES, WX_END)),
        _const_spec((1, LANES)),
        _const_spec((1, LANES)),
        _const_spec((LANES, 1)),
        _const_spec((tm, tm)),
        _const_spec((1, MLA_Q_RANK)),
        _const_spec((1, MLA_KV_RANK)),
        _const_spec((N_GROUP * LANES, MLA_Q_RANK)),
        _const_spec((N_GROUP * LANES, MLA_Q_RANK)),
        _const_spec((MLA_KV_RANK, N_GROUP * LANES)),
        _const_spec((HW, MLA_KV_RANK)),
        pl.BlockSpec((LANES, tm), lambda bi, i: (0, i)),
        pl.BlockSpec((LANES, tm), lambda bi, i: (0, i)),
        pl.BlockSpec((tm, LANES), lambda bi, i: (i, 0)),
        pl.BlockSpec((tm, LANES), lambda bi, i: (i, 0)),
    ]
    out_shape = [kr_shape, qt_shape, vt_shape] * 3 + [jax.ShapeDtypeStruct((b, t, 3 * HW), F32)]
    out_specs = [kr_spec, qt_spec, vt_spec] * 3 + [pl.BlockSpec((1, tm, 3 * HW), lambda bi, i: (bi, i, 0))]
    return pl.pallas_call(
        functools.partial(_proj_kernel, tm=tm),
        out_shape=out_shape,
        grid=(b, nk),
        in_specs=in_specs,
        out_specs=out_specs,
        scratch_shapes=[pltpu.VMEM((8, LANES), F32), pltpu.VMEM((D_MODEL, AL_END), BF16),
                        pltpu.VMEM((NT_END, D_MODEL), BF16), pltpu.VMEM((D_MODEL, WX_END), BF16)],
        compiler_params=pltpu.CompilerParams(dimension_semantics=("arbitrary", "arbitrary"),
                                             vmem_limit_bytes=VMEM_LIMIT),
        name="proj",
    )(x3, g, w_in, sel, fb, pq, pk, tri, qn, kvn, wuqt, wuqrht, wkn, wvt, cosq, sinq, cosk, sink)


def _gate_kernel(qt0_ref, kr_ref, qt_ref, *, t, tq):
    nq = t // tq
    nblk = t // MOBA_BLOCK
    blk = lax.broadcasted_iota(jnp.int32, (nblk, tq), 0)
    lane = lax.broadcasted_iota(jnp.int32, (nblk, LANES), 1)
    for hd in range(N_GROUP):
        ksum = jnp.zeros((nblk, LANES), F32)
        for j in range(nq):
            key_blk = (j * tq + lax.broadcasted_iota(jnp.int32, (nblk, tq), 1)) // MOBA_BLOCK
            ind = jnp.where(key_blk == blk, 1.0, 0.0).astype(BF16)
            ksum = ksum + _dot(ind, kr_ref[0, hd, j * tq:(j + 1) * tq, :])
        qk0 = HEAD_DIM * (hd % 2)
        sel0 = HEAD_DIM - qk0 + SEL_OFF
        is_k = (lane >= qk0) & (lane < qk0 + HEAD_DIM)
        kmean = jnp.where(is_k, ksum * (1.0 / MOBA_BLOCK), 0.0).astype(BF16)
        for qi in range(nq):
            q0 = qt0_ref[0, hd, qi]
            cur = (qi * tq + lax.broadcasted_iota(jnp.int32, (nblk, tq), 1)) // MOBA_BLOCK
            valid = blk < cur
            gate = jnp.where(valid, _dot(kmean, q0), -jnp.inf)
            rank = jnp.zeros((nblk, tq), F32)
            for m in range(nblk):
                gm = gate[m:m + 1, :]
                ahead = (gm > gate) | ((gm == gate) & (blk > m))
                rank = rank + jnp.where(ahead, 1.0, 0.0)
            keep = (valid & (rank < MOBA_TOPK)) | (blk == cur)
            selb = jnp.where(keep, 0.0, NEG).astype(BF16)
            parts = ([q0[:sel0]] if sel0 else []) + [selb, q0[sel0 + nblk:]]
            qt_ref[0, hd, qi] = jnp.concatenate(parts, axis=0)


def _gate_call(qt0, kr):
    b, _, nq, _, tq = qt0.shape
    t = nq * tq
    nblk = t // MOBA_BLOCK
    assert nblk == MOBA_NBLK and SEL_OFF % 16 == 0
    qt_spec = pl.BlockSpec((1, N_GROUP, nq, LANES, tq), lambda bi: (bi, 0, 0, 0, 0))
    kr_spec = pl.BlockSpec((1, N_GROUP, t, LANES), lambda bi: (bi, 0, 0, 0))
    return pl.pallas_call(
        functools.partial(_gate_kernel, t=t, tq=tq),
        out_shape=jax.ShapeDtypeStruct(qt0.shape, BF16),
        grid=(b,),
        in_specs=[qt_spec, kr_spec],
        out_specs=qt_spec,
        compiler_params=pltpu.CompilerParams(dimension_semantics=("arbitrary",),
                                             vmem_limit_bytes=VMEM_LIMIT),
        name="moba_gate",
    )(qt0, kr)


def _attn_kernel(kr_ref, qt_ref, vt_ref, o_ref, acc_ref, *, tq, tk):
    qi = pl.program_id(1)
    ratio = tq // tk
    rowi = lax.broadcasted_iota(jnp.int32, (LANES, tq), 0)
    acc_ref[...] = jnp.zeros(acc_ref.shape, F32)

    def step(j, ms, c0, masked):
        w = tq - c0
        k0 = pl.multiple_of(j * tk, tk)
        sts = [_dot(kr_ref[0, hd, pl.ds(k0, tk), :], qt_ref[0, hd, 0, :, c0:]) for hd in range(N_GROUP)]
        out = []
        for hd in range(N_GROUP):
            st = sts[hd]
            if masked:
                keyi = lax.broadcasted_iota(jnp.int32, (tk, w), 0)
                qryi = lax.broadcasted_iota(jnp.int32, (tk, w), 1)
                st = jnp.where(keyi <= qryi, st, NEG)
            m_old = ms[hd][:, c0:]
            m_new = jnp.maximum(m_old, jnp.max(st, axis=0, keepdims=True))
            p = jnp.exp2(st - m_new)
            pv = _dot(vt_ref[0, hd, j], p.astype(BF16))
            acc_ref[hd, :, c0:] = jnp.exp2(m_old - m_new) * acc_ref[hd, :, c0:] + pv
            out.append(m_new if c0 == 0 else jnp.concatenate([ms[hd][:, :c0], m_new], axis=1))
        return tuple(out)

    def lagged_step(j, carry):
        used, nxt, excess = carry
        k0 = pl.multiple_of(j * tk, tk)
        sts = [_dot(kr_ref[0, hd, pl.ds(k0, tk), :], qt_ref[0, hd, 0]) for hd in range(N_GROUP)]
        new_nxt = []
        for hd in range(N_GROUP):
            st, ref = sts[hd], nxt[hd]
            p = jnp.exp2(st - ref)
            block_max = jnp.max(st, axis=0, keepdims=True)
            pv = _dot(vt_ref[0, hd, j], p.astype(BF16))
            acc_ref[hd] = jnp.exp2(used[hd] - ref) * acc_ref[hd] + pv
            excess = jnp.maximum(excess, block_max - ref)
            new_nxt.append(jnp.maximum(ref, block_max))
        return nxt, tuple(new_nxt), excess

    def diagonal(ms):
        for d in range(ratio):
            ms = step(qi * ratio + d, ms, d * tk, True)
        return ms

    m0 = tuple(jnp.full((1, tq), -jnp.inf, F32) for _ in range(N_GROUP))
    ms = diagonal(m0)
    _, _, excess = lax.fori_loop(0, qi * ratio, lagged_step, (ms, ms, jnp.full((1, tq), -jnp.inf, F32)))

    @pl.when(jnp.max(excess) > LAG_LIMIT)
    def _():
        acc_ref[...] = jnp.zeros(acc_ref.shape, F32)
        diagonal(lax.fori_loop(0, qi * ratio, lambda j, c: step(j, c, 0, False), m0))

    for pair in range(N_GROUP // 2):
        even, odd = acc_ref[2 * pair], acc_ref[2 * pair + 1]
        out_t = jnp.where(rowi < HEAD_DIM, even / even[HEAD_DIM:HEAD_DIM + 1, :], odd / odd[0:1, :])
        o_ref[0, :, pair * LANES:(pair + 1) * LANES] = out_t.T.astype(BF16)


def _attn_call(kr, qt, vt, name):
    b, _, t, _ = kr.shape
    tq, tk = qt.shape[-1], vt.shape[-1]
    return pl.pallas_call(
        functools.partial(_attn_kernel, tq=tq, tk=tk),
        out_shape=jax.ShapeDtypeStruct((b, t, HW), BF16),
        grid=(b, t // tq),
        in_specs=[
            pl.BlockSpec((1, N_GROUP, t, LANES), lambda bi, i: (bi, 0, 0, 0)),
            pl.BlockSpec((1, N_GROUP, 1, LANES, tq), lambda bi, i: (bi, 0, i, 0, 0)),
            pl.BlockSpec((1, N_GROUP, t // tk, LANES, tk), lambda bi, i: (bi, 0, 0, 0, 0)),
        ],
        out_specs=pl.BlockSpec((1, tq, HW), lambda bi, i: (bi, i, 0)),
        scratch_shapes=[pltpu.VMEM((N_GROUP, LANES, tq), F32)],
        compiler_params=pltpu.CompilerParams(dimension_semantics=("arbitrary", "arbitrary"),
                                             vmem_limit_bytes=VMEM_LIMIT),
        name=name,
    )(kr, qt, vt)


DIL_UNROLL = 32


def _dil_kernel(q_ref, k_ref, v_ref, bias_ref, bias2_ref, o_ref, acc_ref, max_ref, *, t):
    span = DIL_SPAN
    c_hd = HEAD_DIM ** -0.5 * LOG2E

    def rows(start, size, dil):
        return pl.ds(start, size) if dil == 1 else pl.ds(start, size, stride=dil)

    def own(shape, hh):
        lane = lax.broadcasted_iota(jnp.int32, shape, 1)
        return (lane < HEAD_DIM) if hh == 0 else (lane >= HEAD_DIM)

    def finish_blocks(work, br, dil):
        for hh, q0, nq, v2, bias, qk in work:
            va = jnp.where(own(v2.shape, hh), v2, 1.0).astype(BF16)
            s = qk + bias
            m = jnp.max(s, axis=1, keepdims=True)
            p = jnp.exp2(s - m)
            acc_ref[br, hh, rows(q0, nq, dil), :] = _dot(p.astype(BF16), va)
            max_ref[br, hh, rows(q0, nq, dil), :] = jnp.broadcast_to(m, (nq, LANES))

    for br, (window, dil) in enumerate(DIL_PAIRS):
        nb = t // dil // span

        def group(it, carry, br=br, dil=dil, nb=nb):
            work = []
            for u in range(DIL_UNROLL):
                idx = it * DIL_UNROLL + u
                r, n = idx // nb, idx % nb
                q0 = r + n * (span * dil)
                first = 1 - jnp.minimum(n, 1)
                k0 = q0 - (1 - first) * (span * dil)
                q2 = q_ref[0, rows(q0, span, dil), :] * c_hd
                k2 = k_ref[0, rows(k0, 2 * span, dil), :].astype(BF16)
                v2 = v_ref[0, rows(k0, 2 * span, dil), :]
                for hh in range(2):
                    qz = jnp.where(own(q2.shape, hh), q2, 0.0).astype(BF16)
                    work.append((hh, q0, span, v2, bias_ref[hh, 2 * br + first], _dot_nt(qz, k2)))
            finish_blocks(work, br, dil)
            return carry

        def group_whole(it, carry, br=br, dil=dil):
            work = []
            for u in range(DIL_UNROLL // 2):
                r = it * (DIL_UNROLL // 2) + u
                q2 = q_ref[0, rows(r, 2 * span, dil), :] * c_hd
                k2 = k_ref[0, rows(r, 2 * span, dil), :].astype(BF16)
                v2 = v_ref[0, rows(r, 2 * span, dil), :]
                for hh in range(2):
                    qz = jnp.where(own(q2.shape, hh), q2, 0.0).astype(BF16)
                    work.append((hh, r, 2 * span, v2, bias2_ref[hh], _dot_nt(qz, k2)))
            finish_blocks(work, br, dil)
            return carry

        if nb == 2:
            lax.fori_loop(0, dil // (DIL_UNROLL // 2), group_whole, 0)
        else:
            lax.fori_loop(0, dil * nb // DIL_UNROLL, group, 0)

    chunk = 256
    lane_c = lax.broadcasted_iota(jnp.int32, (chunk, LANES), 1)

    def finish(ci, carry):
        sl = pl.ds(ci * chunk, chunk)
        outs = []
        for hh in range(2):
            ms = [max_ref[br, hh, sl, :] for br in range(len(DIL_PAIRS))]
            mx = functools.reduce(jnp.maximum, ms)
            tot = sum(acc_ref[br, hh, sl, :] * jnp.exp2(m - mx) for br, m in enumerate(ms))
            outs.append(tot / pltpu.roll(tot, HEAD_DIM, axis=1))
        o_ref[0, sl, :] = jnp.where(lane_c < HEAD_DIM, outs[0], outs[1]).astype(BF16)
        return carry

    lax.fori_loop(0, t // chunk, finish, 0)


def _dil_bias_table():
    qi = np.arange(DIL_SPAN)[:, None]
    kj = np.arange(2 * DIL_SPAN)[None, :]
    tab = np.zeros((N_GROUP, 2 * len(DIL_PAIRS), DIL_SPAN, 2 * DIL_SPAN), np.float32)
    for variant in range(2):
        delta = qi - kj + (DIL_SPAN if variant == 0 else 0)
        valid = (delta >= 0) & (delta <= DIL_SPAN)
        for hd in range(N_GROUP):
            for br, (_, dil) in enumerate(DIL_PAIRS):
                tab[hd, 2 * br + variant] = np.where(valid, -SLOPES_DIL[hd] * LOG2E * (delta * dil), NEG)
    return tab


def _dil_bias_whole(dil):
    delta = np.arange(2 * DIL_SPAN)[:, None] - np.arange(2 * DIL_SPAN)[None, :]
    valid = (delta >= 0) & (delta <= DIL_SPAN)
    return np.stack([np.where(valid, -SLOPES_DIL[hd] * LOG2E * (delta * dil), NEG)
                     for hd in range(N_GROUP)]).astype(np.float32)


def _dil_call(yd, bias, bias2):
    b, t, _ = yd.shape
    for window, dil in DIL_PAIRS:
        nb = t // dil // DIL_SPAN
        assert window // dil == DIL_SPAN and nb * dil * DIL_SPAN == t and nb >= 2
        assert (dil * nb) % DIL_UNROLL == 0
    n_pair = N_GROUP // 2
    n_br = len(DIL_PAIRS)
    return pl.pallas_call(
        functools.partial(_dil_kernel, t=t),
        out_shape=jax.ShapeDtypeStruct((b, t, HW), BF16),
        grid=(b, n_pair),
        in_specs=[
            pl.BlockSpec((1, t, LANES), lambda bi, p: (bi, 0, p)),
            pl.BlockSpec((1, t, LANES), lambda bi, p: (bi, 0, n_pair + p)),
            pl.BlockSpec((1, t, LANES), lambda bi, p: (bi, 0, 2 * n_pair + p)),
            pl.BlockSpec((2, 2 * n_br, DIL_SPAN, 2 * DIL_SPAN), lambda bi, p: (p, 0, 0, 0)),
            pl.BlockSpec((2, 2 * DIL_SPAN, 2 * DIL_SPAN), lambda bi, p: (p, 0, 0)),
        ],
        out_specs=pl.BlockSpec((1, t, LANES), lambda bi, p: (bi, 0, p)),
        scratch_shapes=[pltpu.VMEM((n_br, 2, t, LANES), F32), pltpu.VMEM((n_br, 2, t, LANES), F32)],
        compiler_params=pltpu.CompilerParams(dimension_semantics=("arbitrary", "arbitrary"),
                                             vmem_limit_bytes=VMEM_LIMIT),
        name="dilated",
    )(yd, yd, yd, bias, bias2)


def _cols(w, spec):
    parts = []
    for item in spec:
        if isinstance(item, int):
            parts.append(jnp.zeros((w.shape[0], item), w.dtype))
        else:
            piece = w[:, item[0]:item[0] + item[1]]
            parts.append(-piece if len(item) == 3 else piece)
    return jnp.concatenate(parts, axis=1)


def _layout_specs():
    half = MLA_ROPE // 2
    pad = LANES - HEAD_DIM
    extra = np.zeros((2 * LANES, WX_END), np.float32)
    for base in (0, HEAD_DIM):
        for grp in range(2):
            for hd in range(N_GROUP):
                c0 = WX_F + base + (grp * N_GROUP + hd) * N_PIECE
                extra[hd, c0:c0 + N_PIECE] = 1.0
    for r in range(MLA_ROPE):
        src = LANES + CKR0 - ROPE_SLAB + r
        extra[src, WX_KR + HEAD_DIM + r] = 1.0
        dst = r - half if r >= half else r + half
        extra[src, WX_KRH + HEAD_DIM + dst] = -1.0 if r >= half else 1.0
    qw = HEAD_DIM + MLA_ROPE
    uq = [s for hd in range(N_GROUP) for s in ((hd * qw, qw), LANES - qw)]
    uqrh = [s for hd in range(N_GROUP)
            for s in (HEAD_DIM, (hd * qw + HEAD_DIM + half, half, -1.0), (hd * qw + HEAD_DIM, half), LANES - qw)]
    kn = [s for hd in range(N_GROUP) for s in ((hd * LANES, HEAD_DIM), pad)]
    vv = [(hd * LANES + HEAD_DIM, HEAD_DIM) for hd in range(N_GROUP)]
    q_in, kv_in = N_GROUP * qw, N_GROUP * LANES
    return (extra, _selection(q_in, uq), _selection(q_in, uqrh), _selection(kv_in, kn),
            _selection(kv_in, vv))


def _selection(n_in, spec):
    width = sum(item if isinstance(item, int) else item[1] for item in spec)
    sel = np.zeros((n_in, width), np.float32)
    col = 0
    for item in spec:
        if isinstance(item, int):
            col += item
            continue
        for j in range(item[1]):
            sel[item[0] + j, col + j] = -1.0 if len(item) == 3 else 1.0
        col += item[1]
    return sel


def _relayout(w, sel):
    return jnp.dot(w, jnp.asarray(sel, BF16), preferred_element_type=F32).astype(BF16)


_SPECS = _layout_specs()


def _rope_tables(t):
    inv = ROPE_THETA ** (-jnp.arange(0, MLA_ROPE, 2, dtype=F32) / MLA_ROPE)
    ang = jnp.arange(t, dtype=F32)[:, None] * inv[None, :]
    cos, sin = jnp.cos(ang), jnp.sin(ang)
    z_lo = jnp.zeros((t, HEAD_DIM), F32)
    z_hi = jnp.zeros((t, LANES - HEAD_DIM - MLA_ROPE), F32)
    cosq = jnp.concatenate([jnp.ones((t, HEAD_DIM), F32), cos, cos, z_hi], axis=1)
    sin_tab = jnp.concatenate([z_lo, sin, sin, z_hi], axis=1)
    cosk = jnp.concatenate([z_lo, cos, cos, z_hi], axis=1)
    return cosq.T, sin_tab.T, cosk, sin_tab


def kernel(x, norm_ffn1, ffn1_gate, ffn1_up, ffn1_down, norm_mix, w_in, forget_bias, mla_q_norm,
           mla_kv_norm, w_uq, w_ukv, w_out, norm_ffn2, ffn2_gate, ffn2_up, ffn2_down, norm_final):
    b, t, d = x.shape
    depth = w_in.shape[0]
    assert d == D_MODEL and t % ATT_TQ == 0 and ATT_TQ % TILE == 0 and (b * t) % FFN_TILE == 0
    extra_spec, uq_spec, uqrh_spec, kn_spec, vv_spec = _SPECS
    extra_sel = jnp.asarray(extra_spec, BF16)
    cosq, sinq, cosk, sink = _rope_tables(t)
    tri = jnp.tril(jnp.ones((TILE, TILE), F32)).astype(BF16)
    dil_bias = jnp.asarray(_dil_bias_table())
    whole = [dil for _, dil in DIL_PAIRS if t // dil // DIL_SPAN == 2]
    dil_bias2 = jnp.asarray(_dil_bias_whole(whole[0] if whole else 1))
    lane = np.arange(LANES)
    fg_off = lane % HEAD_DIM
    in_f = fg_off < FG_W
    fg_head = np.where(in_f, (fg_off // N_PIECE) % N_GROUP, 0)
    fg_piece = np.where(in_f, fg_off % N_PIECE, -1).astype(np.int32)
    pq, pk = jnp.asarray(fg_piece[None, :]), jnp.asarray(fg_piece[:, None])
    bf = lambda w: w.astype(BF16)
    row = lambda v: v.reshape(1, -1)

    xf = x.reshape(b * t, d)
    for l in range(depth):
        xf = _ffn_call(xf, None, None, row(norm_ffn1[l]), ffn1_gate, ffn1_up, ffn1_down, l, None)
        fb =jnp.where(jnp.asarray(in_f), jnp.take(forget_bias[l], jnp.asarray(fg_head)), 0.0)[None, :]
        wuq_l, wukv_l = bf(w_uq[l]), bf(w_ukv[l])
        wuqt = _relayout(wuq_l, uq_spec).T
        wuqrht = _relayout(wuq_l, uqrh_spec).T
        wkn = _relayout(wukv_l, kn_spec)
        wvt = _relayout(wukv_l, vv_spec).T
        (kr_a, qt_a0, vt_a, kr_b, qt_b, vt_b, kr_c, qt_c, vt_c, yd) = _proj_call(
            xf.reshape(b, t, d), row(norm_mix[l]), w_in, l, extra_sel, fb, pq, pk, tri,
            row(mla_q_norm[l]), row(mla_kv_norm[l]), wuqt, wuqrht, wkn, wvt, cosq, sinq, cosk, sink)
        qt_a = _gate_call(qt_a0, kr_a)
        o_a = _attn_call(kr_a, qt_a, vt_a, "attn_moba")
        o_b = _attn_call(kr_b, qt_b, vt_b, "attn_forget")
        o_c = _attn_call(kr_c, qt_c, vt_c, "attn_mla")
        o_d = _dil_call(yd, dil_bias, dil_bias2)
        mix = [o.reshape(b * t, HW) for o in (o_a, o_b, o_c, o_d)]
        xf = _ffn_call(xf, mix, bf(w_out[l]), row(norm_ffn2[l]), ffn2_gate, ffn2_up, ffn2_down, l,
                       row(norm_final) if l == depth - 1 else None)
    return xf.reshape(b, t, d)
```

```python
import functools

import numpy as np
import jax
import jax.numpy as jnp
from jax import lax
from jax.experimental import pallas as pl
from jax.experimental.pallas import tpu as pltpu

F32 = jnp.float32
BF16 = jnp.bfloat16

D_MODEL = 1024
HEAD_DIM = 64
N_GROUP = 4
HW = N_GROUP * HEAD_DIM
LANES = 128
MOBA_BLOCK = 256
MOBA_TOPK = 3
MLA_Q_RANK = 256
MLA_KV_RANK = 128
MLA_ROPE = 32
ROPE_THETA = 10000.0
DIL_PAIRS = ((128, 1), (512, 4), (2048, 16))
DIL_SPAN = 128
D_FF = 2816
RMS_EPS = 1e-6
NEG = -1e30
LOG2E = 1.4426950408889634

N_PIECE = 3
MOBA_NBLK = 16
SEL_OFF = 0
ONE_OFF = 16
PIECE_OFF = 19
FGA_OFF = 0
FGB_OFF = N_GROUP * N_PIECE
FG_W = 2 * N_GROUP * N_PIECE

TILE = 512
ATT_TQ = 1024
LAG_LIMIT = 64.0
FFN_TILE = 512
FF_CHUNK = 256
VMEM_LIMIT = 56 * 1024 * 1024

_ALIBI = [2.0 ** (-8.0 * (i + 1) / 8) for i in range(8)]
SLOPES_MOBA = _ALIBI[1::2]
SLOPES_DIL = _ALIBI[0::2]

_WIDTHS = (HW, HW, HW, HW, HW, HW, N_GROUP, MLA_Q_RANK, MLA_KV_RANK, MLA_ROPE, HW, HW, HW)
_OFF = np.concatenate([[0], np.cumsum(_WIDTHS)]).astype(np.int64)
(AQ0, AK0, AV0, BQ0, BK0, BV0, BF0, CQ0, CKV0, CKR0, DQ0, DK0, DV0, IN_WIDTH) = [int(v) for v in _OFF]


def _dot(a, b):
    return jnp.dot(a, b, preferred_element_type=F32)


def _dot_nt(a, b):
    return lax.dot_general(a, b, (((1,), (1,)), ((), ())), preferred_element_type=F32)


def _rms(x, g):
    return x * lax.rsqrt(jnp.mean(x * x, axis=-1, keepdims=True) + RMS_EPS) * g


def _split3(x):
    p0 = x.astype(BF16)
    r = x - p0.astype(F32)
    p1 = r.astype(BF16)
    p2 = (r - p1.astype(F32)).astype(BF16)
    return p0, p1, p2


def _log_sigmoid(x):
    return jnp.minimum(x, 0.0) - jnp.log(1.0 + jnp.exp(-jnp.abs(x)))


def _ffn_kernel(*refs, has_mix, final_norm):
    it = iter(refs)
    x_ref = next(it)
    if has_mix:
        o_refs = [next(it) for _ in range(4)]
        wo_ref = next(it)
    g_ref, wg_ref, wu_ref, wd_ref = next(it), next(it), next(it), next(it)
    gf_ref = next(it) if final_norm else None
    out_ref = next(it)

    x = x_ref[...]
    if has_mix:
        for i, o_ref in enumerate(o_refs):
            x = x + _dot(o_ref[...], wo_ref[i * HW:(i + 1) * HW, :])
    h = _rms(x, g_ref[...]).astype(BF16)
    y = jnp.zeros(x.shape, F32)
    for c in range(D_FF // FF_CHUNK):
        sl = slice(c * FF_CHUNK, (c + 1) * FF_CHUNK)
        gate = _dot(h, wg_ref[:, sl].astype(BF16))
        up = _dot(h, wu_ref[:, sl].astype(BF16))
        act = (gate * jax.nn.sigmoid(gate) * up).astype(BF16)
        y = y + _dot(act, wd_ref[sl, :].astype(BF16))
    x = x + 0.5 * y
    if final_norm:
        x = _rms(x, gf_ref[...])
    out_ref[...] = x


def _const_spec(shape):
    nd = len(shape)
    return pl.BlockSpec(shape, lambda *_: (0,) * nd, pipeline_mode=pl.Buffered(1))


def _layer_spec(shape, layer):
    nd = len(shape)
    return pl.BlockSpec((None,) + tuple(shape), lambda *_: (layer,) + (0,) * nd, pipeline_mode=pl.Buffered(1))


def _ffn_call(x, mix, w_out, g, wg, wu, wd, layer, g_final):
    n = x.shape[0]
    tm = FFN_TILE
    has_mix = mix is not None
    final_norm = g_final is not None
    row = lambda i: (i, 0)
    args, specs = [x], [pl.BlockSpec((tm, D_MODEL), row)]
    if has_mix:
        for o in mix:
            args.append(o)
            specs.append(pl.BlockSpec((tm, HW), row))
        args.append(w_out)
        specs.append(_const_spec((D_MODEL, D_MODEL)))
    args += [g, wg, wu, wd]
    specs += [_const_spec((1, D_MODEL)), _layer_spec((D_MODEL, D_FF), layer), _layer_spec((D_MODEL, D_FF), layer),
              _layer_spec((D_FF, D_MODEL), layer)]
    if final_norm:
        args.append(g_final)
        specs.append(_const_spec((1, D_MODEL)))
    return pl.pallas_call(
        functools.partial(_ffn_kernel, has_mix=has_mix, final_norm=final_norm),
        out_shape=jax.ShapeDtypeStruct((n, D_MODEL), F32),
        grid=(n // tm,),
        in_specs=specs,
        out_specs=pl.BlockSpec((tm, D_MODEL), row),
        compiler_params=pltpu.CompilerParams(dimension_semantics=("arbitrary",),
                                             vmem_limit_bytes=VMEM_LIMIT),
        name="ffn_mix" if has_mix else "ffn",
    )(*args)


PAD_F = -(BF0 + N_GROUP) % LANES
PAD_R = -(CKR0 + MLA_ROPE + PAD_F) % LANES
AL_AK, AL_BK, AL_CQ = AK0, BK0, CQ0 + PAD_F
AL_CKV = CKV0 + PAD_F
AL_D = DQ0 + PAD_F + PAD_R
AL_END = IN_WIDTH + PAD_F + PAD_R
assert all(v % LANES == 0 for v in (AL_AK, AL_BK, AL_CQ, AL_CKV, AL_D, AL_END)) and AL_D - AL_CKV == 2 * LANES
WX_F, WX_KR, WX_KRH, WX_END = 0, LANES, 2 * LANES, 3 * LANES
ROPE_SLAB = CKR0 // LANES * LANES
assert BF0 % LANES == 0 and CKR0 + MLA_ROPE <= ROPE_SLAB + LANES
NT_AQ, NT_AV, NT_BQ, NT_BV, NT_END = AQ0, AV0, BQ0, BV0, BF0


def _pieces_select(idx, base, val):
    p0, p1, p2 = _split3(val)
    return jnp.where(idx == base, p0.astype(F32),
                     jnp.where(idx == base + 1, p1.astype(F32), p2.astype(F32)))


def _pieces_by(which, val):
    p0, p1, p2 = _split3(val)
    return jnp.where(which == 0, p0.astype(F32), jnp.where(which == 1, p1.astype(F32), p2.astype(F32)))


def _vt_with_ones(vt, head):
    row = lax.broadcasted_iota(jnp.int32, vt.shape, 0)
    keep = (row < HEAD_DIM) if head % 2 == 0 else (row >= HEAD_DIM)
    return jnp.where(keep, vt, 1.0).astype(BF16)


def _proj_kernel(x_ref, g_ref, w_ref, sel_ref, fb_ref, pq_ref, pk_ref, tri_ref,
                 qn_ref, kvn_ref, wuqt_ref, wuqrht_ref, wkn_ref, wvt_ref,
                 cosq_ref, sinq_ref, cosk_ref, sink_ref,
                 kr_a, qt_a, vt_a, kr_b, qt_b, vt_b, kr_c, qt_c, vt_c, yd_ref,
                 carry, wal_ref, wnt_ref, wx_ref, *, tm):
    i = pl.program_id(1)

    @pl.when((pl.program_id(0) == 0) & (i == 0))
    def _():
        for src, dst in ((AK0, AL_AK), (BK0, AL_BK)):
            wal_ref[:, dst:dst + HW] = w_ref[:, src:src + HW].astype(BF16)
        wal_ref[:, AL_CQ:AL_CQ + DQ0 - CQ0] = w_ref[:, CQ0:DQ0].astype(BF16)
        wal_ref[:, AL_D - PAD_R:AL_D] = jnp.zeros((D_MODEL, PAD_R), BF16)
        wal_ref[:, AL_D:AL_END] = w_ref[:, DQ0:IN_WIDTH].astype(BF16)
        for src in (NT_AQ, NT_AV, NT_BQ, NT_BV):
            wnt_ref[src:src + HW, :] = w_ref[:, src:src + HW].T.astype(BF16)
        slabs = jnp.concatenate([w_ref[:, BF0:BF0 + LANES], w_ref[:, ROPE_SLAB:ROPE_SLAB + LANES]], axis=1)
        wx_ref[...] = _dot(slabs.astype(BF16), sel_ref[...]).astype(BF16)

    h = _rms(x_ref[0], g_ref[...]).astype(BF16)
    lane = lax.broadcasted_iota(jnp.int32, (tm, LANES), 1)
    rowi = lax.broadcasted_iota(jnp.int32, (LANES, tm), 0)
    tok_r_i = i * tm + lax.broadcasted_iota(jnp.int32, (tm, LANES), 0)
    tok_r = tok_r_i.astype(F32)
    tok_c = (i * tm + lax.broadcasted_iota(jnp.int32, (LANES, tm), 1)).astype(F32)
    c_hd = HEAD_DIM ** -0.5 * LOG2E
    head_sl = lambda hd: slice(hd * LANES, (hd + 1) * LANES)

    @pl.when(i == 0)
    def _():
        carry[...] = jnp.zeros_like(carry)

    def feature_masks(idx, hd):
        qk0 = HEAD_DIM * (hd % 2)
        return (idx >= qk0) & (idx < qk0 + HEAD_DIM), idx - (HEAD_DIM - qk0)

    between = lambda off, lo, n: (off >= lo) & (off < lo + n)

    zk = _dot(h, wal_ref[:, AL_AK:AL_AK + HW])
    zq = _dot_nt(wnt_ref[NT_AQ:NT_AQ + HW, :], h)
    zv = _dot_nt(wnt_ref[NT_AV:NT_AV + HW, :], h)
    for hd in range(N_GROUP):
        slope2 = SLOPES_MOBA[hd] * LOG2E
        pair = head_sl(hd // 2)
        is_k, off = feature_masks(lane, hd)
        ind = between(off, SEL_OFF, MOBA_NBLK) & ((tok_r_i // MOBA_BLOCK) == off - SEL_OFF)
        ext = _pieces_select(off, ONE_OFF, tok_r * slope2)
        kr_a[0, hd] = jnp.where(is_k, zk[:, pair],
                                jnp.where(between(off, ONE_OFF, N_PIECE), ext,
                                          jnp.where(between(off, PIECE_OFF, N_PIECE) | ind, 1.0, 0.0))).astype(BF16)
        is_q, off = feature_masks(rowi, hd)
        ext = _pieces_select(off, PIECE_OFF, tok_c * (-slope2))
        qt_a[0, hd, 0] = jnp.where(is_q, zq[pair, :] * c_hd,
                                   jnp.where(between(off, ONE_OFF, N_PIECE), 1.0,
                                             jnp.where(between(off, PIECE_OFF, N_PIECE), ext, 0.0))).astype(BF16)
        vt_a[0, hd, 0] = _vt_with_ones(zv[pair, :], hd)

    zfr = _dot(h, wx_ref[:, WX_F:WX_KRH])
    in_f = (lane & (HEAD_DIM - 1)) < FG_W
    lf = jnp.where(in_f, _log_sigmoid(zfr[:, :LANES] + fb_ref[...]), 0.0)
    p0, p1, p2 = _split3(lf)
    tri = tri_ref[...]
    dec = _dot(tri, p0) + _dot(tri, p1) + _dot(tri, p2) + carry[0:1, :]
    carry[0:1, :] = dec[tm - 1:tm, :]
    e_k = _pieces_by(pq_ref[...], dec * (-LOG2E))
    e_q = _pieces_by(pk_ref[...], dec.T * LOG2E)
    zk = _dot(h, wal_ref[:, AL_BK:AL_BK + HW])
    zq = _dot_nt(wnt_ref[NT_BQ:NT_BQ + HW, :], h)
    zv = _dot_nt(wnt_ref[NT_BV:NT_BV + HW, :], h)
    for hd in range(N_GROUP):
        pair = head_sl(hd // 2)
        is_k, off = feature_masks(lane, hd)
        kr_b[0, hd] = jnp.where(is_k, zk[:, pair],
                                jnp.where(between(off, FGA_OFF, FG_W // 2), e_k,
                                          jnp.where(between(off, FGB_OFF + N_PIECE * hd, N_PIECE), 1.0, 0.0))
                                ).astype(BF16)
        is_q, off = feature_masks(rowi, hd)
        qt_b[0, hd, 0] = jnp.where(is_q, zq[pair, :] * c_hd,
                                   jnp.where(between(off, FGB_OFF, FG_W // 2), e_q,
                                             jnp.where(between(off, FGA_OFF + N_PIECE * hd, N_PIECE), 1.0, 0.0))
                                   ).astype(BF16)
        vt_b[0, hd, 0] = _vt_with_ones(zv[pair, :], hd)

    c_mla = (HEAD_DIM + MLA_ROPE) ** -0.5 * LOG2E
    cqn = _rms(_dot(h, wal_ref[:, AL_CQ:AL_CQ + MLA_Q_RANK]), qn_ref[...]).astype(BF16)
    zc = _dot(h, wal_ref[:, AL_CKV:AL_D])
    ckvn = _rms(zc[:, :MLA_KV_RANK], kvn_ref[...]).astype(BF16)
    zrh = _dot(h, wx_ref[:, WX_KRH:WX_END])
    rot_k = zfr[:, LANES:] * cosk_ref[...] + zrh * sink_ref[...]
    kn = _dot(ckvn, wkn_ref[...])
    qt = _dot_nt(wuqt_ref[...], cqn)
    qrht = _dot_nt(wuqrht_ref[...], cqn)
    vt = _dot_nt(wvt_ref[...], ckvn)
    cosq, sinq = cosq_ref[...], sinq_ref[...]
    for hd in range(N_GROUP):
        kr_c[0, hd] = (kn[:, head_sl(hd)] + rot_k).astype(BF16)
        qt_c[0, hd, 0] = ((qt[head_sl(hd), :] * cosq + qrht[head_sl(hd), :] * sinq) * c_mla).astype(BF16)
        vt_c[0, hd, 0] = _vt_with_ones(vt[head_sl(hd // 2), :], hd)

    yd_ref[0] = _dot(h, wal_ref[:, AL_D:AL_END])


def _proj_call(x3, g, w_in, layer, sel, fb, pq, pk, tri, qn, kvn, wuqt, wuqrht, wkn, wvt, cosq, sinq, cosk, sink):
    b, t, _ = x3.shape
    tm = TILE
    nk = t // tm
    per_q = ATT_TQ // tm
    kr_shape = jax.ShapeDtypeStruct((b, N_GROUP, t, LANES), BF16)
    vt_shape = jax.ShapeDtypeStruct((b, N_GROUP, nk, LANES, tm), BF16)
    qt_shape = jax.ShapeDtypeStruct((b, N_GROUP, t // ATT_TQ, LANES, ATT_TQ), BF16)
    kr_spec = pl.BlockSpec((1, N_GROUP, tm, LANES), lambda bi, i: (bi, 0, i, 0))
    vt_spec = pl.BlockSpec((1, N_GROUP, 1, LANES, tm), lambda bi, i: (bi, 0, i, 0, 0))
    qt_spec = pl.BlockSpec((1, N_GROUP, 1, LANES, tm), lambda bi, i: (bi, 0, i // per_q, 0, i % per_q))
    in_specs = [
        pl.BlockSpec((1, tm, D_MODEL), lambda bi, i: (bi, i, 0)),
        _const_spec((1, D_MODEL)),
        _layer_spec((D_MODEL, IN_WIDTH), layer),
        _const_spec((2 * LANES, WX_END)),
        _const_spec((1, LANES)),
        _const_spec((1, LANES)),
        _const_spec((LANES, 1)),
        _const_spec((tm, tm)),
        _const_spec((1, MLA_Q_RANK)),
        _const_spec((1, MLA_KV_RANK)),
        _const_spec((N_GROUP * LANES, MLA_Q_RANK)),
        _const_spec((N_GROUP * LANES, MLA_Q_RANK)),
        _const_spec((MLA_KV_RANK, N_GROUP * LANES)),
        _const_spec((HW, MLA_KV_RANK)),
        pl.BlockSpec((LANES, tm), lambda bi, i: (0, i)),
        pl.BlockSpec((LANES, tm), lambda bi, i: (0, i)),
        pl.BlockSpec((tm, LANES), lambda bi, i: (i, 0)),
        pl.BlockSpec((tm, LANES), lambda bi, i: (i, 0)),
    ]
    out_shape = [kr_shape, qt_shape, vt_shape] * 3 + [jax.ShapeDtypeStruct((b, t, 3 * HW), F32)]
    out_specs = [kr_spec, qt_spec, vt_spec] * 3 + [pl.BlockSpec((1, tm, 3 * HW), lambda bi, i: (bi, i, 0))]
    return pl.pallas_call(
        functools.partial(_proj_kernel, tm=tm),
        out_shape=out_shape,
        grid=(b, nk),
        in_specs=in_specs,
        out_specs=out_specs,
        scratch_shapes=[pltpu.VMEM((8, LANES), F32), pltpu.VMEM((D_MODEL, AL_END), BF16),
                        pltpu.VMEM((NT_END, D_MODEL), BF16), pltpu.VMEM((D_MODEL, WX_END), BF16)],
        compiler_params=pltpu.CompilerParams(dimension_semantics=("arbitrary", "arbitrary"),
                                             vmem_limit_bytes=VMEM_LIMIT),
        name="proj",
    )(x3, g, w_in, sel, fb, pq, pk, tri, qn, kvn, wuqt, wuqrht, wkn, wvt, cosq, sinq, cosk, sink)


def _gate_kernel(qt0_ref, kr_ref, qt_ref, *, t, tq):
    nq = t // tq
    nblk = t // MOBA_BLOCK
    blk = lax.broadcasted_iota(jnp.int32, (nblk, tq), 0)
    lane = lax.broadcasted_iota(jnp.int32, (nblk, LANES), 1)
    for hd in range(N_GROUP):
        ksum = jnp.zeros((nblk, LANES), F32)
        for j in range(nq):
            key_blk = (j * tq + lax.broadcasted_iota(jnp.int32, (nblk, tq), 1)) // MOBA_BLOCK
            ind = jnp.where(key_blk == blk, 1.0, 0.0).astype(BF16)
            ksum = ksum + _dot(ind, kr_ref[0, hd, j * tq:(j + 1) * tq, :])
        qk0 = HEAD_DIM * (hd % 2)
        sel0 = HEAD_DIM - qk0 + SEL_OFF
        is_k = (lane >= qk0) & (lane < qk0 + HEAD_DIM)
        kmean = jnp.where(is_k, ksum * (1.0 / MOBA_BLOCK), 0.0).astype(BF16)
        for qi in range(nq):
            q0 = qt0_ref[0, hd, qi]
            cur = (qi * tq + lax.broadcasted_iota(jnp.int32, (nblk, tq), 1)) // MOBA_BLOCK
            valid = blk < cur
            gate = jnp.where(valid, _dot(kmean, q0), -jnp.inf)
            rank = jnp.zeros((nblk, tq), F32)
            for m in range(nblk):
                gm = gate[m:m + 1, :]
                ahead = (gm > gate) | ((gm == gate) & (blk > m))
                rank = rank + jnp.where(ahead, 1.0, 0.0)
            keep = (valid & (rank < MOBA_TOPK)) | (blk == cur)
            selb = jnp.where(keep, 0.0, NEG).astype(BF16)
            parts = ([q0[:sel0]] if sel0 else []) + [selb, q0[sel0 + nblk:]]
            qt_ref[0, hd, qi] = jnp.concatenate(parts, axis=0)


def _gate_call(qt0, kr):
    b, _, nq, _, tq = qt0.shape
    t = nq * tq
    nblk = t // MOBA_BLOCK
    assert nblk == MOBA_NBLK and SEL_OFF % 16 == 0
    qt_spec = pl.BlockSpec((1, N_GROUP, nq, LANES, tq), lambda bi: (bi, 0, 0, 0, 0))
    kr_spec = pl.BlockSpec((1, N_GROUP, t, LANES), lambda bi: (bi, 0, 0, 0))
    return pl.pallas_call(
        functools.partial(_gate_kernel, t=t, tq=tq),
        out_shape=jax.ShapeDtypeStruct(qt0.shape, BF16),
        grid=(b,),
        in_specs=[qt_spec, kr_spec],
        out_specs=qt_spec,
        compiler_params=pltpu.CompilerParams(dimension_semantics=("arbitrary",),
                                             vmem_limit_bytes=VMEM_LIMIT),
        name="moba_gate",
    )(qt0, kr)


def _attn_kernel(kr_ref, qt_ref, vt_ref, o_ref, acc_ref, *gate_scratch, tq, tk, gate):
    qi = pl.program_id(1)
    ratio = tq // tk
    rowi = lax.broadcasted_iota(jnp.int32, (LANES, tq), 0)
    acc_ref[...] = jnp.zeros(acc_ref.shape, F32)

    if gate:
        kmean_ref, qsel_ref = gate_scratch
        t = kr_ref.shape[2]
        nblk = t // MOBA_BLOCK
        blk = lax.broadcasted_iota(jnp.int32, (nblk, tq), 0)

        @pl.when(qi == 0)
        def _():
            lane = lax.broadcasted_iota(jnp.int32, (nblk, LANES), 1)
            for hd in range(N_GROUP):
                ksum = jnp.zeros((nblk, LANES), F32)
                for j in range(t // tq):
                    key_blk = (j * tq + lax.broadcasted_iota(jnp.int32, (nblk, tq), 1)) // MOBA_BLOCK
                    ind = jnp.where(key_blk == blk, 1.0, 0.0).astype(BF16)
                    ksum = ksum + _dot(ind, kr_ref[0, hd, j * tq:(j + 1) * tq, :])
                qk0 = HEAD_DIM * (hd % 2)
                is_k = (lane >= qk0) & (lane < qk0 + HEAD_DIM)
                kmean_ref[hd] = jnp.where(is_k, ksum * (1.0 / MOBA_BLOCK), 0.0).astype(BF16)

        cur = (qi * tq + lax.broadcasted_iota(jnp.int32, (nblk, tq), 1)) // MOBA_BLOCK
        valid = blk < cur
        for hd in range(N_GROUP):
            sel0 = HEAD_DIM - HEAD_DIM * (hd % 2) + SEL_OFF
            q0 = qt_ref[0, hd, 0]
            score = jnp.where(valid, _dot(kmean_ref[hd], q0), -jnp.inf)
            rank = jnp.zeros((nblk, tq), F32)
            for m in range(nblk):
                gm = score[m:m + 1, :]
                ahead = (gm > score) | ((gm == score) & (blk > m))
                rank = rank + jnp.where(ahead, 1.0, 0.0)
            keep = (valid & (rank < MOBA_TOPK)) | (blk == cur)
            selb = jnp.where(keep, 0.0, NEG).astype(BF16)
            parts = ([q0[:sel0]] if sel0 else []) + [selb, q0[sel0 + nblk:]]
            qsel_ref[hd] = jnp.concatenate(parts, axis=0)
        q_cols = lambda hd, c0: qsel_ref[hd, :, c0:]
    else:
        q_cols = lambda hd, c0: qt_ref[0, hd, 0, :, c0:]

    def step(j, ms, c0, masked):
        w = tq - c0
        k0 = pl.multiple_of(j * tk, tk)
        sts = [_dot(kr_ref[0, hd, pl.ds(k0, tk), :], q_cols(hd, c0)) for hd in range(N_GROUP)]
        out = []
        for hd in range(N_GROUP):
            st = sts[hd]
            if masked:
                keyi = lax.broadcasted_iota(jnp.int32, (tk, w), 0)
                qryi = lax.broadcasted_iota(jnp.int32, (tk, w), 1)
                st = jnp.where(keyi <= qryi, st, NEG)
            m_old = ms[hd][:, c0:]
            m_new = jnp.maximum(m_old, jnp.max(st, axis=0, keepdims=True))
            p = jnp.exp2(st - m_new)
            pv = _dot(vt_ref[0, hd, j], p.astype(BF16))
            acc_ref[hd, :, c0:] = jnp.exp2(m_old - m_new) * acc_ref[hd, :, c0:] + pv
            out.append(m_new if c0 == 0 else jnp.concatenate([ms[hd][:, :c0], m_new], axis=1))
        return tuple(out)

    def lagged_step(j, carry):
        used, nxt, excess = carry
        k0 = pl.multiple_of(j * tk, tk)
        sts = [_dot(kr_ref[0, hd, pl.ds(k0, tk), :], q_cols(hd, 0)) for hd in range(N_GROUP)]
        new_nxt = []
        for hd in range(N_GROUP):
            st, ref = sts[hd], nxt[hd]
            p = jnp.exp2(st - ref)
            block_max = jnp.max(st, axis=0, keepdims=True)
            pv = _dot(vt_ref[0, hd, j], p.astype(BF16))
            acc_ref[hd] = jnp.exp2(used[hd] - ref) * acc_ref[hd] + pv
            excess = jnp.maximum(excess, block_max - ref)
            new_nxt.append(jnp.maximum(ref, block_max))
        return nxt, tuple(new_nxt), excess

    def diagonal(ms):
        for d in range(ratio):
            ms = step(qi * ratio + d, ms, d * tk, True)
        return ms

    m0 = tuple(jnp.full((1, tq), -jnp.inf, F32) for _ in range(N_GROUP))
    ms = diagonal(m0)
    _, _, excess = lax.fori_loop(0, qi * ratio, lagged_step, (ms, ms, jnp.full((1, tq), -jnp.inf, F32)))

    @pl.when(jnp.max(excess) > LAG_LIMIT)
    def _():
        acc_ref[...] = jnp.zeros(acc_ref.shape, F32)
        diagonal(lax.fori_loop(0, qi * ratio, lambda j, c: step(j, c, 0, False), m0))

    for pair in range(N_GROUP // 2):
        even, odd = acc_ref[2 * pair], acc_ref[2 * pair + 1]
        out_t = jnp.where(rowi < HEAD_DIM, even / even[HEAD_DIM:HEAD_DIM + 1, :], odd / odd[0:1, :])
        o_ref[0, :, pair * LANES:(pair + 1) * LANES] = out_t.T.astype(BF16)


def _attn_call(kr, qt, vt, name, gate=False):
    b, _, t, _ = kr.shape
    tq, tk = qt.shape[-1], vt.shape[-1]
    scratch = [pltpu.VMEM((N_GROUP, LANES, tq), F32)]
    if gate:
        assert t // MOBA_BLOCK == MOBA_NBLK and SEL_OFF % 16 == 0
        scratch += [pltpu.VMEM((N_GROUP, MOBA_NBLK, LANES), BF16), pltpu.VMEM((N_GROUP, LANES, tq), BF16)]
    return pl.pallas_call(
        functools.partial(_attn_kernel, tq=tq, tk=tk, gate=gate),
        out_shape=jax.ShapeDtypeStruct((b, t, HW), BF16),
        grid=(b, t // tq),
        in_specs=[
            pl.BlockSpec((1, N_GROUP, t, LANES), lambda bi, i: (bi, 0, 0, 0)),
            pl.BlockSpec((1, N_GROUP, 1, LANES, tq), lambda bi, i: (bi, 0, i, 0, 0)),
            pl.BlockSpec((1, N_GROUP, t // tk, LANES, tk), lambda bi, i: (bi, 0, 0, 0, 0)),
        ],
        out_specs=pl.BlockSpec((1, tq, HW), lambda bi, i: (bi, i, 0)),
        scratch_shapes=scratch,
        compiler_params=pltpu.CompilerParams(dimension_semantics=("arbitrary", "arbitrary"),
                                             vmem_limit_bytes=VMEM_LIMIT),
        name=name,
    )(kr, qt, vt)


DIL_UNROLL = 32


def _dil_kernel(q_ref, k_ref, v_ref, bias_ref, bias2_ref, o_ref, acc_ref, max_ref, *, t):
    span = DIL_SPAN
    c_hd = HEAD_DIM ** -0.5 * LOG2E

    def rows(start, size, dil):
        return pl.ds(start, size) if dil == 1 else pl.ds(start, size, stride=dil)

    def own(shape, hh):
        lane = lax.broadcasted_iota(jnp.int32, shape, 1)
        return (lane < HEAD_DIM) if hh == 0 else (lane >= HEAD_DIM)

    def finish_blocks(work, br, dil):
        for hh, q0, nq, v2, bias, qk in work:
            va = jnp.where(own(v2.shape, hh), v2, 1.0).astype(BF16)
            s = qk + bias
            m = jnp.max(s, axis=1, keepdims=True)
            p = jnp.exp2(s - m)
            acc_ref[br, hh, rows(q0, nq, dil), :] = _dot(p.astype(BF16), va)
            max_ref[br, hh, rows(q0, nq, dil), :] = jnp.broadcast_to(m, (nq, LANES))

    for br, (window, dil) in enumerate(DIL_PAIRS):
        nb = t // dil // span

        def group(it, carry, br=br, dil=dil, nb=nb):
            work = []
            for u in range(DIL_UNROLL):
                idx = it * DIL_UNROLL + u
                r, n = idx // nb, idx % nb
                q0 = r + n * (span * dil)
                first = 1 - jnp.minimum(n, 1)
                k0 = q0 - (1 - first) * (span * dil)
                q2 = q_ref[0, rows(q0, span, dil), :] * c_hd
                k2 = k_ref[0, rows(k0, 2 * span, dil), :].astype(BF16)
                v2 = v_ref[0, rows(k0, 2 * span, dil), :]
                for hh in range(2):
                    qz = jnp.where(own(q2.shape, hh), q2, 0.0).astype(BF16)
                    work.append((hh, q0, span, v2, bias_ref[hh, 2 * br + first], _dot_nt(qz, k2)))
            finish_blocks(work, br, dil)
            return carry

        def group_whole(it, carry, br=br, dil=dil):
            work = []
            for u in range(DIL_UNROLL // 2):
                r = it * (DIL_UNROLL // 2) + u
                q2 = q_ref[0, rows(r, 2 * span, dil), :] * c_hd
                k2 = k_ref[0, rows(r, 2 * span, dil), :].astype(BF16)
                v2 = v_ref[0, rows(r, 2 * span, dil), :]
                for hh in range(2):
                    qz = jnp.where(own(q2.shape, hh), q2, 0.0).astype(BF16)
                    work.append((hh, r, 2 * span, v2, bias2_ref[hh], _dot_nt(qz, k2)))
            finish_blocks(work, br, dil)
            return carry

        if nb == 2:
            lax.fori_loop(0, dil // (DIL_UNROLL // 2), group_whole, 0)
        else:
            lax.fori_loop(0, dil * nb // DIL_UNROLL, group, 0)

    chunk = 256
    lane_c = lax.broadcasted_iota(jnp.int32, (chunk, LANES), 1)

    def finish(ci, carry):
        sl = pl.ds(ci * chunk, chunk)
        outs = []
        for hh in range(2):
            ms = [max_ref[br, hh, sl, :] for br in range(len(DIL_PAIRS))]
            mx = functools.reduce(jnp.maximum, ms)
            tot = sum(acc_ref[br, hh, sl, :] * jnp.exp2(m - mx) for br, m in enumerate(ms))
            outs.append(tot / pltpu.roll(tot, HEAD_DIM, axis=1))
        o_ref[0, sl, :] = jnp.where(lane_c < HEAD_DIM, outs[0], outs[1]).astype(BF16)
        return carry

    lax.fori_loop(0, t // chunk, finish, 0)


def _dil_bias_table():
    qi = np.arange(DIL_SPAN)[:, None]
    kj = np.arange(2 * DIL_SPAN)[None, :]
    tab = np.zeros((N_GROUP, 2 * len(DIL_PAIRS), DIL_SPAN, 2 * DIL_SPAN), np.float32)
    for variant in range(2):
        delta = qi - kj + (DIL_SPAN if variant == 0 else 0)
        valid = (delta >= 0) & (delta <= DIL_SPAN)
        for hd in range(N_GROUP):
            for br, (_, dil) in enumerate(DIL_PAIRS):
                tab[hd, 2 * br + variant] = np.where(valid, -SLOPES_DIL[hd] * LOG2E * (delta * dil), NEG)
    return tab


def _dil_bias_whole(dil):
    delta = np.arange(2 * DIL_SPAN)[:, None] - np.arange(2 * DIL_SPAN)[None, :]
    valid = (delta >= 0) & (delta <= DIL_SPAN)
    return np.stack([np.where(valid, -SLOPES_DIL[hd] * LOG2E * (delta * dil), NEG)
                     for hd in range(N_GROUP)]).astype(np.float32)


def _dil_call(yd, bias, bias2):
    b, t, _ = yd.shape
    for window, dil in DIL_PAIRS:
        nb = t // dil // DIL_SPAN
        assert window // dil == DIL_SPAN and nb * dil * DIL_SPAN == t and nb >= 2
        assert (dil * nb) % DIL_UNROLL == 0
    n_pair = N_GROUP // 2
    n_br = len(DIL_PAIRS)
    return pl.pallas_call(
        functools.partial(_dil_kernel, t=t),
        out_shape=jax.ShapeDtypeStruct((b, t, HW), BF16),
        grid=(b, n_pair),
        in_specs=[
            pl.BlockSpec((1, t, LANES), lambda bi, p: (bi, 0, p)),
            pl.BlockSpec((1, t, LANES), lambda bi, p: (bi, 0, n_pair + p)),
            pl.BlockSpec((1, t, LANES), lambda bi, p: (bi, 0, 2 * n_pair + p)),
            pl.BlockSpec((2, 2 * n_br, DIL_SPAN, 2 * DIL_SPAN), lambda bi, p: (p, 0, 0, 0)),
            pl.BlockSpec((2, 2 * DIL_SPAN, 2 * DIL_SPAN), lambda bi, p: (p, 0, 0)),
        ],
        out_specs=pl.BlockSpec((1, t, LANES), lambda bi, p: (bi, 0, p)),
        scratch_shapes=[pltpu.VMEM((n_br, 2, t, LANES), F32), pltpu.VMEM((n_br, 2, t, LANES), F32)],
        compiler_params=pltpu.CompilerParams(dimension_semantics=("arbitrary", "arbitrary"),
                                             vmem_limit_bytes=VMEM_LIMIT),
        name="dilated",
    )(yd, yd, yd, bias, bias2)


def _cols(w, spec):
    parts = []
    for item in spec:
        if isinstance(item, int):
            parts.append(jnp.zeros((w.shape[0], item), w.dtype))
        else:
            piece = w[:, item[0]:item[0] + item[1]]
            parts.append(-piece if len(item) == 3 else piece)
    return jnp.concatenate(parts, axis=1)


def _layout_specs():
    half = MLA_ROPE // 2
    pad = LANES - HEAD_DIM
    extra = np.zeros((2 * LANES, WX_END), np.float32)
    for base in (0, HEAD_DIM):
        for grp in range(2):
            for hd in range(N_GROUP):
                c0 = WX_F + base + (grp * N_GROUP + hd) * N_PIECE
                extra[hd, c0:c0 + N_PIECE] = 1.0
    for r in range(MLA_ROPE):
        src = LANES + CKR0 - ROPE_SLAB + r
        extra[src, WX_KR + HEAD_DIM + r] = 1.0
        dst = r - half if r >= half else r + half
        extra[src, WX_KRH + HEAD_DIM + dst] = -1.0 if r >= half else 1.0
    qw = HEAD_DIM + MLA_ROPE
    uq = [s for hd in range(N_GROUP) for s in ((hd * qw, qw), LANES - qw)]
    uqrh = [s for hd in range(N_GROUP)
            for s in (HEAD_DIM, (hd * qw + HEAD_DIM + half, half, -1.0), (hd * qw + HEAD_DIM, half), LANES - qw)]
    kn = [s for hd in range(N_GROUP) for s in ((hd * LANES, HEAD_DIM), pad)]
    vv = [(hd * LANES + HEAD_DIM, HEAD_DIM) for hd in range(N_GROUP)]
    q_in, kv_in = N_GROUP * qw, N_GROUP * LANES
    return (extra, _selection(q_in, uq), _selection(q_in, uqrh), _selection(kv_in, kn),
            _selection(kv_in, vv))


def _selection(n_in, spec):
    width = sum(item if isinstance(item, int) else item[1] for item in spec)
    sel = np.zeros((n_in, width), np.float32)
    col = 0
    for item in spec:
        if isinstance(item, int):
            col += item
            continue
        for j in range(item[1]):
            sel[item[0] + j, col + j] = -1.0 if len(item) == 3 else 1.0
        col += item[1]
    return sel


def _relayout(w, sel):
    return jnp.dot(w, jnp.asarray(sel, BF16), preferred_element_type=F32).astype(BF16)


_SPECS = _layout_specs()


def _rope_tables(t):
    inv = ROPE_THETA ** (-jnp.arange(0, MLA_ROPE, 2, dtype=F32) / MLA_ROPE)
    ang = jnp.arange(t, dtype=F32)[:, None] * inv[None, :]
    cos, sin = jnp.cos(ang), jnp.sin(ang)
    z_lo = jnp.zeros((t, HEAD_DIM), F32)
    z_hi = jnp.zeros((t, LANES - HEAD_DIM - MLA_ROPE), F32)
    cosq = jnp.concatenate([jnp.ones((t, HEAD_DIM), F32), cos, cos, z_hi], axis=1)
    sin_tab = jnp.concatenate([z_lo, sin, sin, z_hi], axis=1)
    cosk = jnp.concatenate([z_lo, cos, cos, z_hi], axis=1)
    return cosq.T, sin_tab.T, cosk, sin_tab


def kernel(x, norm_ffn1, ffn1_gate, ffn1_up, ffn1_down, norm_mix, w_in, forget_bias, mla_q_norm,
           mla_kv_norm, w_uq, w_ukv, w_out, norm_ffn2, ffn2_gate, ffn2_up, ffn2_down, norm_final):
    b, t, d = x.shape
    depth = w_in.shape[0]
    assert d == D_MODEL and t % ATT_TQ == 0 and ATT_TQ % TILE == 0 and (b * t) % FFN_TILE == 0
    extra_spec, uq_spec, uqrh_spec, kn_spec, vv_spec = _SPECS
    extra_sel = jnp.asarray(extra_spec, BF16)
    cosq, sinq, cosk, sink = _rope_tables(t)
    tri = jnp.tril(jnp.ones((TILE, TILE), F32)).astype(BF16)
    dil_bias = jnp.asarray(_dil_bias_table())
    whole = [dil for _, dil in DIL_PAIRS if t // dil // DIL_SPAN == 2]
    dil_bias2 = jnp.asarray(_dil_bias_whole(whole[0] if whole else 1))
    lane = np.arange(LANES)
    fg_off = lane % HEAD_DIM
    in_f = fg_off < FG_W
    fg_head = np.where(in_f, (fg_off // N_PIECE) % N_GROUP, 0)
    fg_piece = np.where(in_f, fg_off % N_PIECE, -1).astype(np.int32)
    pq, pk = jnp.asarray(fg_piece[None, :]), jnp.asarray(fg_piece[:, None])
    bf = lambda w: w.astype(BF16)
    row = lambda v: v.reshape(1, -1)

    xf = x.reshape(b * t, d)
    for l in range(depth):
        xf = _ffn_call(xf, None, None, row(norm_ffn1[l]), ffn1_gate, ffn1_up, ffn1_down, l, None)
        fb =jnp.where(jnp.asarray(in_f), jnp.take(forget_bias[l], jnp.asarray(fg_head)), 0.0)[None, :]
        wuq_l, wukv_l = bf(w_uq[l]), bf(w_ukv[l])
        wuqt = _relayout(wuq_l, uq_spec).T
        wuqrht = _relayout(wuq_l, uqrh_spec).T
        wkn = _relayout(wukv_l, kn_spec)
        wvt = _relayout(wukv_l, vv_spec).T
        (kr_a, qt_a0, vt_a, kr_b, qt_b, vt_b, kr_c, qt_c, vt_c, yd) = _proj_call(
            xf.reshape(b, t, d), row(norm_mix[l]), w_in, l, extra_sel, fb, pq, pk, tri,
            row(mla_q_norm[l]), row(mla_kv_norm[l]), wuqt, wuqrht, wkn, wvt, cosq, sinq, cosk, sink)
        o_a = _attn_call(kr_a, qt_a0, vt_a, "attn_moba", gate=True)
        o_b = _attn_call(kr_b, qt_b, vt_b, "attn_forget")
        o_c = _attn_call(kr_c, qt_c, vt_c, "attn_mla")
        o_d = _dil_call(yd, dil_bias, dil_bias2)
        mix = [o.reshape(b * t, HW) for o in (o_a, o_b, o_c, o_d)]
        xf = _ffn_call(xf, mix, bf(w_out[l]), row(norm_ffn2[l]), ffn2_gate, ffn2_up, ffn2_down, l,
                       row(norm_final) if l == depth - 1 else None)
    return xf.reshape(b, t, d)
```
